```python
import jax
import jax.numpy as jnp
from jax import lax
import numpy as np

D_MODEL = 1024
BATCH = 4
SEQ = 8192
DEPTH = 4

GRID_W = 64
CTX_LEN = 256
EPS = 1e-6
N_MOD = 6

MLA_HEADS = 8
MLA_Q_RANK = 256
MLA_KV_RANK = 128
MLA_NOPE = 64
MLA_ROPE = 32
MLA_V = 64
MLA_WIDTH = MLA_HEADS * MLA_V
MLA_SCALE = (MLA_NOPE + MLA_ROPE) ** -0.5
ROPE_BASE = 10000.0
Q_BLOCK = 128

FOURIER_GROUPS = 4
FOURIER_CH = 64
FOURIER_WIDTH = FOURIER_GROUPS * FOURIER_CH

HGRN_HEADS = 4
HGRN_EXPAND = 128
HGRN_HEAD_V = 64
HGRN_KEY_WIDTH = HGRN_HEADS * HGRN_EXPAND
HGRN_WIDTH = HGRN_HEADS * HGRN_HEAD_V
CHUNK = 64

MIX_WIDTH = MLA_WIDTH + FOURIER_WIDTH + HGRN_WIDTH
IN_SIZES = (MLA_Q_RANK, MLA_KV_RANK, MLA_ROPE, FOURIER_WIDTH, HGRN_KEY_WIDTH, HGRN_KEY_WIDTH, HGRN_KEY_WIDTH, HGRN_WIDTH, HGRN_WIDTH)
IN_WIDTH = sum(IN_SIZES)

D_FF = 2816
N_EXPERTS = 8
TOP_K = 2
N_DENSE = (DEPTH + 1) // 2
N_MOE = DEPTH // 2

kernel_name = 'hybrid_mla_fourier_hgrn2_moe_dit'

F32 = jnp.float32


def normalize_f32(x):
    xf = x.astype(F32)
    return xf * lax.rsqrt(jnp.mean(xf * xf, axis=-1, keepdims=True) + EPS)


def rms_norm(x, gain):
    return (normalize_f32(x) * gain.astype(F32)).astype(x.dtype)


def modulate(x, shift, scale):
    return (normalize_f32(x) * (1.0 + scale.astype(F32)) + shift.astype(F32)).astype(x.dtype)


def split_columns(p):
    out, start = [], 0
    for size in IN_SIZES:
        out.append(p[..., start:start + size])
        start += size
    return out


def axial_rope_tables(n_tokens):
    rows = n_tokens // GRID_W
    row_ids = jnp.repeat(jnp.arange(rows, dtype=F32), GRID_W)
    col_ids = jnp.tile(jnp.arange(GRID_W, dtype=F32), rows)
    per_axis = MLA_ROPE // 2
    inv_freq = ROPE_BASE ** (-jnp.arange(0, per_axis, 2, dtype=F32) / per_axis)
    ang = jnp.concatenate([row_ids[:, None] * inv_freq, col_ids[:, None] * inv_freq], axis=-1)
    return jnp.cos(ang), jnp.sin(ang)


def apply_axial_rope(x, cos, sin):
    quarter = MLA_ROPE // 4
    bshape = (1, x.shape[1]) + (1,) * (x.ndim - 3) + (2, quarter)
    xf = x.astype(F32).reshape(x.shape[:-1] + (2, 2, quarter))
    x1, x2 = xf[..., 0, :], xf[..., 1, :]
    c, s = cos.reshape(bshape), sin.reshape(bshape)
    out = jnp.stack([x1 * c - x2 * s, x1 * s + x2 * c], axis=-2)
    return out.reshape(x.shape).astype(x.dtype)


def mla_queries(dq, q_norm_g, w_uq, rope):
    cq = rms_norm(dq, q_norm_g)
    q = (cq @ w_uq).reshape(dq.shape[:2] + (MLA_HEADS, MLA_NOPE + MLA_ROPE))
    qn, qr = q[..., :MLA_NOPE], q[..., MLA_NOPE:]
    if rope is not None:
        qr = apply_axial_rope(qr, *rope)
    return qn, qr


def mla_keys_values(dkv, kr, kv_norm_g, w_ukv, rope):
    ckv = rms_norm(dkv, kv_norm_g)
    kv = (ckv @ w_ukv).reshape(dkv.shape[:2] + (MLA_HEADS, MLA_NOPE + MLA_V))
    kn, v = kv[..., :MLA_NOPE], kv[..., MLA_NOPE:]
    if rope is not None:
        kr = apply_axial_rope(kr, *rope)
    return kn, kr, v


def block_attention(qn, qr, kn, kr, v):
    B, T = qn.shape[:2]
    nb = T // Q_BLOCK

    def to_blocks(a):
        return a.reshape((B, nb, Q_BLOCK) + a.shape[2:]).swapaxes(0, 1)

    def one_block(blk):
        qn_b, qr_b = blk
        s = jnp.einsum('bqhd,bkhd->bhqk', qn_b, kn) + jnp.einsum('bqhr,bkr->bhqk', qr_b, kr)
        p = jax.nn.softmax(s.astype(F32) * MLA_SCALE, axis=-1)
        return jnp.einsum('bhqk,bkhd->bqhd', p.astype(v.dtype), v)

    o = lax.map(one_block, (to_blocks(qn), to_blocks(qr)))
    return o.swapaxes(0, 1).reshape(B, T, MLA_HEADS * MLA_V)


def fourier_mix(u):
    B, T, _ = u.shape
    uf = u.astype(F32).reshape(B, T, FOURIER_GROUPS, FOURIER_CH)
    y = jnp.fft.fft2(uf, axes=(1, 3), norm='ortho').real
    return y.reshape(B, T, FOURIER_WIDTH).astype(u.dtype)


def hgrn_heads(a, d):
    return a.reshape(a.shape[:2] + (HGRN_HEADS, d))


def hgrn_lower_bounds(lb_logits):
    p = jax.nn.softmax(lb_logits.astype(F32), axis=1)
    return jnp.cumsum(p, axis=1) - p[:, :1]


def hgrn_forget(f_raw, lb):
    f = lb + (1.0 - lb) * jax.nn.sigmoid(f_raw.astype(F32))
    return hgrn_heads(1.0 - f, HGRN_EXPAND), hgrn_heads(jnp.log(f), HGRN_EXPAND)


def chunk_scan(k, log_f, v, s0, q):
    B, T, H, _ = k.shape
    n = T // CHUNK

    def blocks(a):
        return a.reshape(B, n, CHUNK, H, a.shape[-1]).transpose(1, 0, 3, 2, 4)

    def state_update(S, kb, G, vb):
        G_end = G[:, :, -1:, :]
        return jnp.exp(G_end[:, :, 0, :])[..., None] * S + jnp.einsum('bhsd,bhsv->bhdv', kb * jnp.exp(G_end - G), vb)

    if q is None:
        def state_step(S, xs):
            kb, gb, vb = xs
            return state_update(S, kb, jnp.cumsum(gb, axis=2), vb), None
        S_fin, _ = lax.scan(state_step, s0, (blocks(k), blocks(log_f), blocks(v)))
        return None, S_fin

    lower = jnp.tril(jnp.ones((CHUNK, CHUNK), dtype=bool))[:, :, None]

    def step(S, xs):
        qb, kb, gb, vb = xs
        G = jnp.cumsum(gb, axis=2)
        diff = G[:, :, :, None, :] - G[:, :, None, :, :]
        decay = jnp.exp(jnp.where(lower, diff, -jnp.inf))
        scores = jnp.einsum('bhtd,bhsd,bhtsd->bhts', qb, kb, decay)
        o = jnp.einsum('bhts,bhsv->bhtv', scores, vb) + jnp.einsum('bhtd,bhdv->bhtv', qb * jnp.exp(G), S)
        return state_update(S, kb, G, vb), o

    S_fin, o = lax.scan(step, s0, (blocks(q), blocks(k), blocks(log_f), blocks(v)))
    o = o.transpose(1, 0, 3, 2, 4).reshape(B, T, H, v.shape[-1])
    return o, S_fin


def hgrn_direction(q, k, log_f, v, s0, reverse):
    if reverse:
        k, log_f, v = jnp.flip(k, 1), jnp.flip(log_f, 1), jnp.flip(v, 1)
        q = None if q is None else jnp.flip(q, 1)
    o, s = chunk_scan(k, log_f, v, s0, q)
    if reverse and o is not None:
        o = jnp.flip(o, 1)
    return o, s


def hgrn_readout(o, g_raw, gain):
    o = normalize_f32(o) * gain.astype(F32)
    o = o * jax.nn.silu(hgrn_heads(g_raw.astype(F32), HGRN_HEAD_V))
    return o.reshape(o.shape[:2] + (HGRN_WIDTH,)).astype(g_raw.dtype)


def token_mix(h_lat, h_ctx, w_in, q_norm_g, kv_norm_g, w_uq, w_ukv, hgrn_g, lb_fwd, lb_bwd, rope, need_ctx):
    B = h_lat.shape[0]
    dq_l, dkv_l, kr_l, fo_l, hq_l, hff_l, hfb_l, hi_l, hg_l = split_columns(h_lat @ w_in)
    dq_c, dkv_c, kr_c, fo_c, hq_c, hff_c, hfb_c, hi_c, hg_c = split_columns(h_ctx @ w_in)

    kn_c, krr_c, v_c = mla_keys_values(dkv_c, kr_c, kv_norm_g, w_ukv, None)
    kn_l, krr_l, v_l = mla_keys_values(dkv_l, kr_l, kv_norm_g, w_ukv, rope)
    qn_l, qr_l = mla_queries(dq_l, q_norm_g, w_uq, rope)
    att_l = block_attention(qn_l, qr_l, jnp.concatenate([kn_c, kn_l], axis=1),
                            jnp.concatenate([krr_c, krr_l], axis=1), jnp.concatenate([v_c, v_l], axis=1))

    four_l = fourier_mix(fo_l)

    kf_c, gf_c = hgrn_forget(hff_c, lb_fwd)
    kb_c, gb_c = hgrn_forget(hfb_c, lb_bwd)
    v_hc = hgrn_heads(hi_c.astype(F32), HGRN_HEAD_V)
    q_hc = hgrn_heads(jax.nn.silu(hq_c.astype(F32)), HGRN_EXPAND) if need_ctx else None
    zero = jnp.zeros((B, HGRN_HEADS, HGRN_EXPAND, HGRN_HEAD_V), F32)
    oc_f, sc_f = hgrn_direction(q_hc, kf_c, gf_c, v_hc, zero, False)
    oc_b, sc_b = hgrn_direction(q_hc, kb_c, gb_c, v_hc, zero, True)

    kf_l, gf_l = hgrn_forget(hff_l, lb_fwd)
    kb_l, gb_l = hgrn_forget(hfb_l, lb_bwd)
    v_hl = hgrn_heads(hi_l.astype(F32), HGRN_HEAD_V)
    q_hl = hgrn_heads(jax.nn.silu(hq_l.astype(F32)), HGRN_EXPAND)
    ol_f, _ = hgrn_direction(q_hl, kf_l, gf_l, v_hl, sc_f, False)
    ol_b, _ = hgrn_direction(q_hl, kb_l, gb_l, v_hl, sc_b, True)
    rec_l = hgrn_readout(ol_f + ol_b, hg_l, hgrn_g)

    mix_lat = jnp.concatenate([att_l, four_l, rec_l.astype(att_l.dtype)], axis=-1)
    if not need_ctx:
        return mix_lat, None

    qn_c, qr_c = mla_queries(dq_c, q_norm_g, w_uq, None)
    att_c = block_attention(qn_c, qr_c, kn_c, krr_c, v_c)
    four_c = fourier_mix(fo_c)
    rec_c = hgrn_readout(oc_f + oc_b, hg_c, hgrn_g)
    mix_ctx = jnp.concatenate([att_c, four_c, rec_c.astype(att_c.dtype)], axis=-1)
    return mix_lat, mix_ctx


def swiglu(h, w_gate, w_up, w_down):
    return (jax.nn.silu(h @ w_gate) * (h @ w_up)) @ w_down


def moe_swiglu(h, w_router, w_gate, w_up, w_down):
    logits = jnp.einsum('btd,de->bte', h, w_router).astype(F32)
    top_v, top_i = lax.top_k(logits, TOP_K)
    top_w = jax.nn.softmax(top_v, axis=-1)
    gates = jnp.sum(jax.nn.one_hot(top_i, N_EXPERTS, dtype=F32) * top_w[..., None], axis=-2)
    y = jnp.zeros(h.shape, F32)
    for e in range(N_EXPERTS):
        y = y + gates[..., e:e + 1] * swiglu(h, w_gate[e], w_up[e], w_down[e]).astype(F32)
    return y.astype(h.dtype)


def setup_inputs(seed: int = 0) -> dict:
    key = jax.random.key(seed)
    ks = jax.random.split(key, 26)

    def nrm(k, shape, scale):
        return scale * jax.random.normal(k, shape, F32)

    return {
        'x': nrm(ks[0], (BATCH, SEQ, D_MODEL), 1.0),
        'c': nrm(ks[1], (BATCH, D_MODEL), 1.0),
        'ctx': nrm(ks[2], (BATCH, CTX_LEN, D_MODEL), 1.0),
        'c_ctx': nrm(ks[3], (D_MODEL,), 1.0),
        'w_ada': nrm(ks[4], (DEPTH, D_MODEL, N_MOD * D_MODEL), 0.5 * D_MODEL ** -0.5),
        'b_ada': nrm(ks[5], (DEPTH, N_MOD * D_MODEL), 0.02),
        'w_in': nrm(ks[6], (DEPTH, D_MODEL, IN_WIDTH), D_MODEL ** -0.5),
        'q_norm_g': 1.0 + nrm(ks[7], (DEPTH, MLA_Q_RANK), 0.02),
        'kv_norm_g': 1.0 + nrm(ks[8], (DEPTH, MLA_KV_RANK), 0.02),
        'w_uq': nrm(ks[9], (DEPTH, MLA_Q_RANK, MLA_HEADS * (MLA_NOPE + MLA_ROPE)), MLA_Q_RANK ** -0.5),
        'w_ukv': nrm(ks[10], (DEPTH, MLA_KV_RANK, MLA_HEADS * (MLA_NOPE + MLA_V)), MLA_KV_RANK ** -0.5),
        'hgrn_norm_g': 1.0 + nrm(ks[11], (DEPTH, HGRN_HEAD_V), 0.02),
        'lb_logits': nrm(ks[12], (2, DEPTH, HGRN_KEY_WIDTH), 0.5),
        'w_out': nrm(ks[13], (DEPTH, MIX_WIDTH, D_MODEL), MIX_WIDTH ** -0.5),
        'ffn_w_gate': nrm(ks[14], (N_DENSE, D_MODEL, D_FF), D_MODEL ** -0.5),
        'ffn_w_up': nrm(ks[15], (N_DENSE, D_MODEL, D_FF), D_MODEL ** -0.5),
        'ffn_w_down': nrm(ks[16], (N_DENSE, D_FF, D_MODEL), D_FF ** -0.5),
        'moe_router': nrm(ks[17], (N_MOE, D_MODEL, N_EXPERTS), D_MODEL ** -0.5),
        'moe_w_gate': nrm(ks[18], (N_MOE, N_EXPERTS, D_MODEL, D_FF), D_MODEL ** -0.5),
        'moe_w_up': nrm(ks[19], (N_MOE, N_EXPERTS, D_MODEL, D_FF), D_MODEL ** -0.5),
        'moe_w_down': nrm(ks[20], (N_MOE, N_EXPERTS, D_FF, D_MODEL), D_FF ** -0.5),
        'final_g': 1.0 + nrm(ks[21], (D_MODEL,), 0.02),
    }


def reference(x, c, ctx, c_ctx, w_ada, b_ada, w_in, q_norm_g, kv_norm_g, w_uq, w_ukv, hgrn_norm_g, lb_logits,
              w_out, ffn_w_gate, ffn_w_up, ffn_w_down, moe_router, moe_w_gate, moe_w_up, moe_w_down, final_g):
    dt = x.dtype
    n_lat = x.shape[1]
    n_ctx = ctx.shape[1]
    rope = axial_rope_tables(n_lat)
    lower_bounds = hgrn_lower_bounds(lb_logits)
    cond_lat = jax.nn.silu(c.astype(F32))
    cond_ctx = jax.nn.silu(c_ctx.astype(F32))

    for layer in range(DEPTH):
        need_ctx = layer < DEPTH - 1
        wa = w_ada[layer].astype(F32)
        ba = b_ada[layer].astype(F32)
        m_lat = jnp.split((cond_lat @ wa + ba)[:, None, :], N_MOD, axis=-1)
        m_ctx = jnp.split(cond_ctx @ wa + ba, N_MOD, axis=-1)

        h_lat = modulate(x, m_lat[0], m_lat[1])
        h_ctx = modulate(ctx, m_ctx[0], m_ctx[1])
        mix_lat, mix_ctx = token_mix(h_lat, h_ctx, w_in[layer], q_norm_g[layer], kv_norm_g[layer], w_uq[layer],
                                     w_ukv[layer], hgrn_norm_g[layer], lower_bounds[0, layer],
                                     lower_bounds[1, layer], rope, need_ctx)
        x = (x + m_lat[2] * (mix_lat @ w_out[layer])).astype(dt)
        if need_ctx:
            ctx = (ctx + m_ctx[2] * (mix_ctx @ w_out[layer])).astype(ctx.dtype)

        h = modulate(x, m_lat[3], m_lat[4])
        if need_ctx:
            h = jnp.concatenate([modulate(ctx, m_ctx[3], m_ctx[4]), h], axis=1)
        if layer % 2 == 0:
            j = layer // 2
            y = swiglu(h, ffn_w_gate[j], ffn_w_up[j], ffn_w_down[j])
        else:
            j = layer // 2
            y = moe_swiglu(h, moe_router[j], moe_w_gate[j], moe_w_up[j], moe_w_down[j])
        if need_ctx:
            ctx = (ctx + m_ctx[5] * y[:, :n_ctx]).astype(ctx.dtype)
            x = (x + m_lat[5] * y[:, n_ctx:]).astype(dt)
        else:
            x = (x + m_lat[5] * y).astype(dt)

    return rms_norm(x, final_g)
```

```python
import functools

import numpy as np
import jax
import jax.numpy as jnp
from jax import lax
from jax.experimental import pallas as pl
from jax.experimental.pallas import tpu as pltpu

F32 = jnp.float32
BF16 = jnp.bfloat16
HIGHEST = lax.Precision.HIGHEST

EPS = 1e-6
N_MOD = 6
GRID_W = 64
ROPE_BASE = 10000.0

MLA_HEADS = 8
MLA_Q_RANK = 256
MLA_KV_RANK = 128
MLA_NOPE = 64
MLA_ROPE = 32
MLA_V = 64
MLA_SCALE = (MLA_NOPE + MLA_ROPE) ** -0.5
HEAD_PAD = 128

FOURIER_GROUPS = 4
FOURIER_CH = 64
FOURIER_WIDTH = FOURIER_GROUPS * FOURIER_CH

HGRN_HEADS = 4
HGRN_EXPAND = 128
HGRN_HEAD_V = 64
HGRN_KEY_WIDTH = HGRN_HEADS * HGRN_EXPAND
HGRN_WIDTH = HGRN_HEADS * HGRN_HEAD_V
CHUNK = 64
SUB = 16
N_SUB = CHUNK // SUB

N_EXPERTS = 8
TOP_K = 2

IN_SIZES = (MLA_Q_RANK, MLA_KV_RANK, MLA_ROPE, FOURIER_WIDTH, HGRN_KEY_WIDTH, HGRN_KEY_WIDTH, HGRN_KEY_WIDTH,
            HGRN_WIDTH, HGRN_WIDTH)

V7X_VMEM_LIMIT = 56 * 1024 * 1024


def _cparams(*sem):
    return pltpu.CompilerParams(dimension_semantics=sem, vmem_limit_bytes=V7X_VMEM_LIMIT)


def _silu(x):
    return x * (1.0 / (1.0 + jnp.exp(-x)))


def _rms(x):
    return x * lax.rsqrt(jnp.mean(x * x, axis=-1, keepdims=True) + EPS)


def _ada_kernel(c_ref, w_ref, b_ref, o_ref):
    c = _silu(c_ref[...])
    o_ref[0] = jnp.dot(c, w_ref[0], precision=HIGHEST, preferred_element_type=F32) + b_ref[0]


def ada_mods(cond, w_ada, b_ada):
    depth, d, n = w_ada.shape
    g = cond.shape[0]
    tn = 1536
    return pl.pallas_call(
        _ada_kernel,
        grid=(depth, n // tn),
        in_specs=[pl.BlockSpec((g, d), lambda l, j: (0, 0)),
                  pl.BlockSpec((1, d, tn), lambda l, j: (l, 0, j)),
                  pl.BlockSpec((1, 1, tn), lambda l, j: (l, 0, j))],
        out_specs=pl.BlockSpec((1, g, tn), lambda l, j: (l, 0, j)),
        out_shape=jax.ShapeDtypeStruct((depth, g, n), F32),
        compiler_params=_cparams("arbitrary", "arbitrary"),
        name="ada_mods",
    )(cond, w_ada, b_ada.reshape(depth, 1, n))


C_DQ = 0
C_DKV = C_DQ + MLA_Q_RANK
C_KRA = C_DKV + MLA_KV_RANK
C_KRB = C_KRA + HEAD_PAD
C_FO = C_KRB + HEAD_PAD
C_HQ = C_FO + FOURIER_WIDTH
C_HF = C_HQ + HGRN_KEY_WIDTH
C_HB = C_HF + HGRN_KEY_WIDTH
C_HI = C_HB + HGRN_KEY_WIDTH
C_HG = C_HI + HGRN_WIDTH
C_END = C_HG + HGRN_WIDTH


def _rope_partner():
    i = np.arange(MLA_ROPE)
    half = (i % 16) // 8
    return np.where(half == 0, i + 8, i - 8)


def prep_in_weights(w_in, w_uq, w_ukv):
    d = w_in.shape[0]
    starts = np.cumsum((0,) + IN_SIZES)
    dq, dkv, kr, fo, hq, hf, hb, hi, hg = [w_in[:, starts[i]:starts[i + 1]] for i in range(len(IN_SIZES))]
    z64 = jnp.zeros((d, MLA_NOPE), w_in.dtype)
    z32 = jnp.zeros((d, HEAD_PAD - MLA_NOPE - MLA_ROPE), w_in.dtype)
    kra = jnp.concatenate([z64, kr, z32], axis=1)
    krb = jnp.concatenate([z64, kr[:, _rope_partner()], z32], axis=1)
    w_ext = jnp.concatenate([dq, dkv, kra, krb, fo, hq, hf, hb, hi, hg], axis=1).astype(BF16)

    r = w_uq.shape[0]
    wq = w_uq.reshape(r, MLA_HEADS, MLA_NOPE + MLA_ROPE)
    zq = jnp.zeros((r, MLA_HEADS, HEAD_PAD - MLA_NOPE - MLA_ROPE), w_uq.dtype)
    wqa = jnp.concatenate([wq, zq], axis=2).reshape(r, MLA_HEADS * HEAD_PAD)
    wq_sw = jnp.concatenate([jnp.zeros((r, MLA_HEADS, MLA_NOPE), w_uq.dtype),
                             wq[:, :, MLA_NOPE:][:, :, _rope_partner()], zq], axis=2)
    wqb = wq_sw.reshape(r, MLA_HEADS * HEAD_PAD)
    wq_ext = jnp.concatenate([wqa, wqb], axis=1).astype(BF16)

    rk = w_ukv.shape[0]
    wkv = w_ukv.reshape(rk, MLA_HEADS, MLA_NOPE + MLA_V)
    zk = jnp.zeros((rk, MLA_HEADS, HEAD_PAD - MLA_NOPE), w_ukv.dtype)
    wk = jnp.concatenate([wkv[:, :, :MLA_NOPE], zk], axis=2).reshape(rk, MLA_HEADS * HEAD_PAD)
    zv = jnp.zeros((rk, MLA_HEADS, HEAD_PAD - MLA_V), w_ukv.dtype)
    wv = jnp.concatenate([wkv[:, :, MLA_NOPE:], zv], axis=2).reshape(rk, MLA_HEADS * HEAD_PAD)
    wkv_ext = jnp.concatenate([wk, wv], axis=1).astype(BF16)
    return w_ext, wq_ext, wkv_ext


def rope_tables(n_tokens):
    rows = n_tokens // GRID_W
    row_ids = jnp.repeat(jnp.arange(rows, dtype=F32), GRID_W)
    col_ids = jnp.tile(jnp.arange(GRID_W, dtype=F32), rows)
    per_axis = MLA_ROPE // 2
    inv_freq = ROPE_BASE ** (-jnp.arange(0, per_axis, 2, dtype=F32) / per_axis)
    ang = jnp.concatenate([row_ids[:, None] * inv_freq, col_ids[:, None] * inv_freq], axis=-1)
    cos, sin = jnp.cos(ang), jnp.sin(ang)
    i = np.arange(MLA_ROPE)
    tab_idx = (i // 16) * 8 + (i % 8)
    sign = np.where((i % 16) // 8 == 0, -1.0, 1.0).astype(np.float32)
    c32 = cos[:, tab_idx]
    s32 = sin[:, tab_idx] * sign
    ones = jnp.ones((n_tokens, MLA_NOPE), F32)
    zeros = jnp.zeros((n_tokens, MLA_NOPE), F32)
    pad = jnp.zeros((n_tokens, HEAD_PAD - MLA_NOPE - MLA_ROPE), F32)
    ck = jnp.concatenate([ones, c32, pad], axis=1)
    sk = jnp.concatenate([zeros, s32, pad], axis=1)
    return ck * MLA_SCALE, sk * MLA_SCALE, ck, sk


def identity_rope_tables(n_tokens):
    lane = np.arange(HEAD_PAD)
    ck = jnp.asarray(np.broadcast_to((lane < MLA_NOPE + MLA_ROPE).astype(np.float32), (n_tokens, HEAD_PAD)))
    sk = jnp.zeros((n_tokens, HEAD_PAD), F32)
    return ck * MLA_SCALE, sk, ck, sk


def _in_kernel(x_ref, mod_ref, w_ref, gq_ref, gkv_ref, wq_ref, wkv_ref, cq_ref, sq_ref, ck_ref, sk_ref, vone_ref,
               q_out, k_out, v_out, fo_out, hq_out, hf_out, hb_out, hi_out, hg_out):
    x = x_ref[0]
    shift = mod_ref[0, 0:1, :]
    scale = mod_ref[0, 1:2, :]
    h = (_rms(x) * (1.0 + scale) + shift).astype(BF16)

    def proj(c0, c1):
        return jnp.dot(h, w_ref[:, c0:c1], preferred_element_type=F32)

    fo_out[0] = proj(C_FO, C_HQ).astype(fo_out.dtype)
    hq_out[0] = proj(C_HQ, C_HF)
    hf_out[0] = proj(C_HF, C_HB)
    hb_out[0] = proj(C_HB, C_HI)
    hi_out[0] = proj(C_HI, C_HG)
    hg_out[0] = proj(C_HG, C_END)

    hw = MLA_HEADS * HEAD_PAD
    cq = (_rms(proj(C_DQ, C_DKV)) * gq_ref[...]).astype(BF16)
    qa = jnp.dot(cq, wq_ref[:, :hw], preferred_element_type=F32)
    qb = jnp.dot(cq, wq_ref[:, hw:], preferred_element_type=F32)
    ckv = (_rms(proj(C_DKV, C_KRA)) * gkv_ref[...]).astype(BF16)
    kn = jnp.dot(ckv, wkv_ref[:, :hw], preferred_element_type=F32)
    vv = jnp.dot(ckv, wkv_ref[:, hw:], preferred_element_type=F32)
    kr = proj(C_KRA, C_KRB) * ck_ref[...] + proj(C_KRB, C_FO) * sk_ref[...]
    cqt = cq_ref[...]
    sqt = sq_ref[...]
    for hd in range(MLA_HEADS):
        sl = slice(hd * HEAD_PAD, (hd + 1) * HEAD_PAD)
        q_out[0, :, sl] = (qa[:, sl] * cqt + qb[:, sl] * sqt).astype(q_out.dtype)
        k_out[0, :, sl] = (kn[:, sl] + kr).astype(k_out.dtype)
    v_out[0] = (vv + vone_ref[...]).astype(v_out.dtype)


def in_proj(x, mods, group_of_batch, w_ext, gq, gkv, wq_ext, wkv_ext, tabs, tm):
    b, t, d = x.shape
    hw = MLA_HEADS * HEAD_PAD
    cq, sq, ck, sk = tabs
    lane = np.arange(hw) % HEAD_PAD
    vone = jnp.asarray((lane == MLA_V).astype(np.float32)).reshape(1, hw)
    grid = (b, t // tm)
    tok = lambda w: pl.BlockSpec((1, tm, w), lambda i, j: (i, j, 0))
    full2 = lambda a: pl.BlockSpec(a.shape, lambda i, j: (0, 0))
    tab = pl.BlockSpec((tm, HEAD_PAD), lambda i, j: (j, 0))
    if group_of_batch:
        mod_spec = pl.BlockSpec((1, N_MOD, d), lambda i, j: (i, 0, 0))
    else:
        mod_spec = pl.BlockSpec((1, N_MOD, d), lambda i, j: (0, 0, 0))
    outs = [(hw, BF16), (hw, BF16), (hw, BF16), (FOURIER_WIDTH, F32), (HGRN_KEY_WIDTH, F32), (HGRN_KEY_WIDTH, F32),
            (HGRN_KEY_WIDTH, F32), (HGRN_WIDTH, F32), (HGRN_WIDTH, F32)]
    return pl.pallas_call(
        _in_kernel,
        grid=grid,
        in_specs=[tok(d), mod_spec, full2(w_ext), full2(gq), full2(gkv), full2(wq_ext), full2(wkv_ext),
                  tab, tab, tab, tab, full2(vone)],
        out_specs=[tok(w) for w, _ in outs],
        out_shape=[jax.ShapeDtypeStruct((b, t, w), dt) for w, dt in outs],
        compiler_params=_cparams("arbitrary", "arbitrary"),
        name="in_proj",
    )(x, mods, w_ext, gq, gkv, wq_ext, wkv_ext, cq, sq, ck, sk, vone)


def _attn_kernel(*refs, tk, n_lat):
    if n_lat:
        q_ref, kc_ref, vc_ref, kl_ref, vl_ref, o_ref = refs
    else:
        q_ref, kc_ref, vc_ref, o_ref = refs
    q = q_ref[0]
    tq = q.shape[0]
    nt = (((1,), (1,)), ((), ()))

    def block(k, v, m, acc):
        s = lax.dot_general(q, k, nt, preferred_element_type=F32)
        m_new = jnp.maximum(m, jnp.max(s, axis=-1, keepdims=True))
        p = jnp.exp(s - m_new)
        acc = acc * jnp.exp(m - m_new) + jnp.dot(p.astype(BF16), v, preferred_element_type=F32)
        return m_new, acc

    m0 = jnp.full((tq, 1), -1e30, F32)
    acc0 = jnp.zeros((tq, HEAD_PAD), F32)
    m, acc = block(kc_ref[0], vc_ref[0], m0, acc0)
    if n_lat:
        def body(i, carry):
            start = pl.multiple_of(i * tk, tk)
            return block(kl_ref[0, pl.ds(start, tk), :], vl_ref[0, pl.ds(start, tk), :], *carry)
        m, acc = lax.fori_loop(0, n_lat // tk, body, (m, acc))
    denom = acc[:, MLA_V:MLA_V + 1]
    o_ref[0] = (acc * (1.0 / denom)).astype(o_ref.dtype)


def attention(q, k_ctx, v_ctx, k_lat=None, v_lat=None, tq=512, tk=1024):
    b, t, hw = q.shape
    n_ctx = k_ctx.shape[1]
    n_lat = 0 if k_lat is None else k_lat.shape[1]
    tq = min(tq, t)
    grid = (b, MLA_HEADS, t // tq)
    qspec = pl.BlockSpec((1, tq, HEAD_PAD), lambda i, h, j: (i, j, h))
    cspec = pl.BlockSpec((1, n_ctx, HEAD_PAD), lambda i, h, j: (i, 0, h))
    in_specs = [qspec, cspec, cspec]
    args = [q, k_ctx, v_ctx]
    if n_lat:
        lspec = pl.BlockSpec((1, n_lat, HEAD_PAD), lambda i, h, j: (i, 0, h))
        in_specs += [lspec, lspec]
        args += [k_lat, v_lat]
    return pl.pallas_call(
        functools.partial(_attn_kernel, tk=tk, n_lat=n_lat),
        grid=grid,
        in_specs=in_specs,
        out_specs=qspec,
        out_shape=jax.ShapeDtypeStruct((b, t, hw), BF16),
        compiler_params=_cparams("arbitrary", "arbitrary", "arbitrary"),
        name="attention",
    )(*args)


def _channel_dft_mats(norm):
    c = np.arange(FOURIER_CH)
    ang = 2.0 * np.pi * np.outer(c, c) / FOURIER_CH
    eye = np.eye(FOURIER_GROUPS)
    cc = np.kron(eye, np.cos(ang)) * norm
    sc = np.kron(eye, np.sin(ang)) * norm
    return np.concatenate([cc, sc], axis=0).astype(np.float32)


def _dft_small_kernel(u_ref, f_ref, cs_ref, o_ref):
    t = u_ref.shape[1]
    x = jnp.dot(f_ref[...], u_ref[0].astype(BF16), preferred_element_type=F32)
    w = FOURIER_WIDTH
    y = jnp.dot(x[:t].astype(BF16), cs_ref[:w], preferred_element_type=F32)
    y = y + jnp.dot(x[t:].astype(BF16), cs_ref[w:], preferred_element_type=F32)
    o_ref[0] = y.astype(o_ref.dtype)


def fourier_small(u):
    b, t, w = u.shape
    n = np.arange(t)
    ang = 2.0 * np.pi * (np.outer(n, n) % t) / t
    f = jnp.asarray(np.concatenate([np.cos(ang), -np.sin(ang)], axis=0), BF16)
    cs = jnp.asarray(_channel_dft_mats((t * FOURIER_CH) ** -0.5), BF16)
    return pl.pallas_call(
        _dft_small_kernel,
        grid=(b,),
        in_specs=[pl.BlockSpec((1, t, w), lambda i: (i, 0, 0)),
                  pl.BlockSpec(f.shape, lambda i: (0, 0)),
                  pl.BlockSpec(cs.shape, lambda i: (0, 0))],
        out_specs=pl.BlockSpec((1, t, w), lambda i: (i, 0, 0)),
        out_shape=jax.ShapeDtypeStruct((b, t, w), BF16),
        compiler_params=_cparams("arbitrary"),
        name="fourier_ctx",
    )(u, f, cs)


def _dft_stage1_kernel(x_ref, f_ref, o_ref):
    o_ref[0] = jnp.dot(f_ref[...], x_ref[0].astype(BF16), preferred_element_type=F32).astype(o_ref.dtype)


def _dft_stage2_kernel(z_ref, m_ref, cs_ref, o_ref, *, n2):
    w = FOURIER_WIDTH
    for j in range(z_ref.shape[1]):
        x = jnp.dot(m_ref[j], z_ref[0, j], preferred_element_type=F32)
        y = jnp.dot(x[:n2].astype(BF16), cs_ref[:w], preferred_element_type=F32)
        y = y + jnp.dot(x[n2:].astype(BF16), cs_ref[w:], preferred_element_type=F32)
        o_ref[0, j] = y.astype(o_ref.dtype)


def fourier_large(u, n1=128):
    bsz, t, w = u.shape
    n2 = t // n1
    a = np.arange(n1)
    ang1 = 2.0 * np.pi * np.outer(a, a) / n1
    f1 = np.empty((2 * n1, n1), np.float64)
    f1[0::2] = np.cos(ang1)
    f1[1::2] = -np.sin(ang1)
    f1 = jnp.asarray(f1, BF16)
    k1 = np.arange(n1)[:, None, None]
    k2 = np.arange(n2)[None, :, None]
    bb = np.arange(n2)[None, None, :]
    ang2 = 2.0 * np.pi * ((bb * (k1 + n1 * k2)) % t) / t
    mr, mi = np.cos(ang2), np.sin(ang2)
    m = np.concatenate([np.concatenate([mr, mi], axis=2), np.concatenate([-mi, mr], axis=2)], axis=1)
    m = jnp.asarray(m, BF16)
    cs = jnp.asarray(_channel_dft_mats((t * FOURIER_CH) ** -0.5), BF16)

    cols = n2 * w
    tc = 2048
    z = pl.pallas_call(
        _dft_stage1_kernel,
        grid=(bsz, cols // tc),
        in_specs=[pl.BlockSpec((1, n1, tc), lambda i, j: (i, 0, j)),
                  pl.BlockSpec(f1.shape, lambda i, j: (0, 0))],
        out_specs=pl.BlockSpec((1, 2 * n1, tc), lambda i, j: (i, 0, j)),
        out_shape=jax.ShapeDtypeStruct((bsz, 2 * n1, cols), BF16),
        compiler_params=_cparams("arbitrary", "arbitrary"),
        name="fourier_stage1",
    )(u.reshape(bsz, n1, cols), f1)
    z = z.reshape(bsz, n1, 2 * n2, w)
    kc = 16
    y = pl.pallas_call(
        functools.partial(_dft_stage2_kernel, n2=n2),
        grid=(bsz, n1 // kc),
        in_specs=[pl.BlockSpec((1, kc, 2 * n2, w), lambda i, j: (i, j, 0, 0)),
                  pl.BlockSpec((kc, 2 * n2, 2 * n2), lambda i, j: (j, 0, 0)),
                  pl.BlockSpec(cs.shape, lambda i, j: (0, 0))],
        out_specs=pl.BlockSpec((1, kc, n2, w), lambda i, j: (i, j, 0, 0)),
        out_shape=jax.ShapeDtypeStruct((bsz, n1, n2, w), BF16),
        compiler_params=_cparams("arbitrary", "arbitrary"),
        name="fourier_stage2",
    )(z, m, cs)
    return y.transpose(0, 2, 1, 3).reshape(bsz, t, w)


def _scan_consts():
    r = np.arange(CHUNK)
    ltri = (r[None, :] <= r[:, None]).astype(np.float32)
    e = np.zeros((SUB * HGRN_EXPAND, HEAD_PAD), np.float32)
    for s in range(SUB):
        e[s * HGRN_EXPAND:(s + 1) * HGRN_EXPAND, s:CHUNK:SUB] = 1.0
    return jnp.asarray(ltri), jnp.asarray(e, BF16)


def _scan_kernel(hq_ref, hf_ref, hi_ref, lb_ref, s0_ref, ltri_ref, e_ref, o_ref, sfin_ref, s_scr, *, n_chunks):
    step = pl.program_id(1)

    @pl.when(step == 0)
    def _():
        s_scr[...] = s0_ref[0]

    lb = lb_ref[...]
    ltri = ltri_ref[...]
    row = lax.broadcasted_iota(jnp.int32, (CHUNK, CHUNK), 0)
    col = lax.broadcasted_iota(jnp.int32, (CHUNK, CHUNK), 1)
    sub_shift = SUB.bit_length() - 1
    same_sub = (col >> sub_shift) == (row >> sub_shift)
    diag_mask = same_sub & ((col & (SUB - 1)) <= (row & (SUB - 1)))
    sub_row = lax.broadcasted_iota(jnp.int32, (N_SUB, SUB, HGRN_EXPAND), 1)
    col_low = lax.broadcasted_iota(jnp.int32, (SUB, CHUNK), 1)
    nt = (((1,), (1,)), ((), ()))
    tn = (((0,), (0,)), ((), ()))

    def chunk(c, carry):
        r0 = pl.multiple_of(c * CHUNK, CHUNK)
        f = lb + (1.0 - lb) * (1.0 / (1.0 + jnp.exp(-hf_ref[0, pl.ds(r0, CHUNK), :])))
        logf = jnp.log(f)
        kk = 1.0 - f
        g = jnp.dot(ltri, logf, precision=HIGHEST, preferred_element_type=F32)
        q = _silu(hq_ref[0, pl.ds(r0, CHUNK), :])
        v = hi_ref[0, pl.ds(r0, CHUNK), :]
        outs = []
        for hd in range(HGRN_HEADS):
            ks = slice(hd * HGRN_EXPAND, (hd + 1) * HGRN_EXPAND)
            vs = slice(hd * HGRN_HEAD_V, (hd + 1) * HGRN_HEAD_V)
            gh, qh, kh, vh = g[:, ks], q[:, ks], kk[:, ks], v[:, vs]
            vb = vh.astype(BF16)
            st_prev = s_scr[hd]
            g_end = gh[CHUNK - 1:CHUNK, :]
            o = lax.dot_general((qh * jnp.exp(gh)).astype(BF16), st_prev.astype(BF16), nt,
                                preferred_element_type=F32)
            k_dec = (kh * jnp.exp(g_end - gh)).astype(BF16)
            upd = lax.dot_general(vb, k_dec, tn, preferred_element_type=F32)
            s_scr[hd] = jnp.exp(g_end) * st_prev + upd
            g3 = gh.reshape(N_SUB, SUB, HGRN_EXPAND)
            k3 = kh.reshape(N_SUB, SUB, HGRN_EXPAND)
            q3 = qh.reshape(N_SUB, SUB, HGRN_EXPAND)
            slots = []
            for j in range(SUB):
                gj = jnp.broadcast_to(g3[:, j:j + 1, :], g3.shape)
                kj = jnp.broadcast_to(k3[:, j:j + 1, :], k3.shape)
                dec = jnp.exp(jnp.where(sub_row >= j, g3 - gj, -jnp.inf))
                slots.append((q3 * kj * dec).reshape(CHUNK, HGRN_EXPAND).astype(BF16))
            pmat = jnp.concatenate(slots, axis=1)
            a = jnp.dot(pmat, e_ref[...], preferred_element_type=F32)[:, :CHUNK]
            a = jnp.where(diag_mask, a, 0.0)
            parts = [a[0:SUB]]
            for i in range(1, N_SUB):
                ref_g = gh[i * SUB - 1:i * SUB, :]
                qi = (qh[i * SUB:(i + 1) * SUB] * jnp.exp(gh[i * SUB:(i + 1) * SUB] - ref_g)).astype(BF16)
                kp = (kh * jnp.exp(jnp.minimum(ref_g - gh, 0.0))).astype(BF16)
                low = lax.dot_general(qi, kp, nt, preferred_element_type=F32)
                parts.append(a[i * SUB:(i + 1) * SUB] + jnp.where(col_low < i * SUB, low, 0.0))
            amat = jnp.concatenate(parts, axis=0)
            o = o + jnp.dot(amat.astype(BF16), vb, preferred_element_type=F32)
            outs.append(o)
        o_ref[0, pl.ds(r0, CHUNK), :] = jnp.concatenate(outs, axis=1)
        return carry

    lax.fori_loop(0, n_chunks, chunk, 0)

    @pl.when(step == pl.num_programs(1) - 1)
    def _():
        sfin_ref[0] = s_scr[...]


def hgrn_scan(hq, hf, hi, lb, s0, tl=512):
    b, t, kw = hq.shape
    tl = min(tl, t)
    ltri, e = _scan_consts()
    tok = lambda w: pl.BlockSpec((1, tl, w), lambda i, j: (i, j, 0))
    sspec = pl.BlockSpec((1, HGRN_HEADS, HGRN_HEAD_V, HGRN_EXPAND), lambda i, j: (i, 0, 0, 0))
    return pl.pallas_call(
        functools.partial(_scan_kernel, n_chunks=tl // CHUNK),
        grid=(b, t // tl),
        in_specs=[tok(kw), tok(kw), tok(HGRN_WIDTH),
                  pl.BlockSpec(lb.shape, lambda i, j: (0, 0)), sspec,
                  pl.BlockSpec(ltri.shape, lambda i, j: (0, 0)),
                  pl.BlockSpec(e.shape, lambda i, j: (0, 0))],
        out_specs=[tok(HGRN_WIDTH), sspec],
        out_shape=[jax.ShapeDtypeStruct((b, t, HGRN_WIDTH), F32),
                   jax.ShapeDtypeStruct((b, HGRN_HEADS, HGRN_HEAD_V, HGRN_EXPAND), F32)],
        scratch_shapes=[pltpu.VMEM((HGRN_HEADS, HGRN_HEAD_V, HGRN_EXPAND), F32)],
        compiler_params=_cparams("arbitrary", "arbitrary"),
        name="hgrn_scan",
    )(hq, hf, hi, lb, s0, ltri, e)


def prep_out_weights(w_out):
    d = w_out.shape[1]
    wa = w_out[:MLA_HEADS * MLA_V].reshape(MLA_HEADS, MLA_V, d)
    wa = jnp.concatenate([wa, jnp.zeros((MLA_HEADS, HEAD_PAD - MLA_V, d), w_out.dtype)], axis=1)
    wa = wa.reshape(MLA_HEADS * HEAD_PAD, d)
    rest = w_out[MLA_HEADS * MLA_V:]
    return jnp.concatenate([wa, rest], axis=0).astype(BF16)


def _out_kernel(x_ref, att_ref, four_ref, of_ref, ob_ref, hg_ref, mod_ref, w_ref, gm_ref, gain_ref, x_out, h_out):
    hw = MLA_HEADS * HEAD_PAD
    o = of_ref[0] + ob_ref[0]
    ms = jnp.dot((o * o).astype(BF16), gm_ref[...], preferred_element_type=F32)
    rec = o * lax.rsqrt(ms + EPS) * gain_ref[...] * _silu(hg_ref[0])
    y = jnp.dot(att_ref[0], w_ref[:hw], preferred_element_type=F32)
    y = y + jnp.dot(four_ref[0], w_ref[hw:hw + FOURIER_WIDTH], preferred_element_type=F32)
    y = y + jnp.dot(rec.astype(BF16), w_ref[hw + FOURIER_WIDTH:], preferred_element_type=F32)
    x1 = x_ref[0] + mod_ref[0, 2:3, :] * y
    x_out[0] = x1
    h_out[0] = (_rms(x1) * (1.0 + mod_ref[0, 4:5, :]) + mod_ref[0, 3:4, :]).astype(h_out.dtype)


def out_proj(x, att, four, o_f, o_b, hg, mods, group_of_batch, w_out_ext, gain, tm):
    b, t, d = x.shape
    hw = MLA_HEADS * HEAD_PAD
    gm = jnp.asarray(np.kron(np.eye(HGRN_HEADS), np.full((HGRN_HEAD_V, HGRN_HEAD_V), 1.0 / HGRN_HEAD_V)), BF16)
    gain_t = jnp.tile(gain.astype(F32), HGRN_HEADS).reshape(1, HGRN_WIDTH)
    tok = lambda w: pl.BlockSpec((1, tm, w), lambda i, j: (i, j, 0))
    full2 = lambda a: pl.BlockSpec(a.shape, lambda i, j: (0, 0))
    if group_of_batch:
        mod_spec = pl.BlockSpec((1, N_MOD, d), lambda i, j: (i, 0, 0))
    else:
        mod_spec = pl.BlockSpec((1, N_MOD, d), lambda i, j: (0, 0, 0))
    return pl.pallas_call(
        _out_kernel,
        grid=(b, t // tm),
        in_specs=[tok(d), tok(hw), tok(FOURIER_WIDTH), tok(HGRN_WIDTH), tok(HGRN_WIDTH), tok(HGRN_WIDTH), mod_spec,
                  full2(w_out_ext), full2(gm), full2(gain_t)],
        out_specs=[tok(d), tok(d)],
        out_shape=[jax.ShapeDtypeStruct((b, t, d), F32), jax.ShapeDtypeStruct((b, t, d), BF16)],
        compiler_params=_cparams("arbitrary", "arbitrary"),
        name="out_proj",
    )(x, att, four, o_f, o_b, hg, mods, w_out_ext, gm, gain_t)


def _ffn_kernel(h_ref, x_ref, mod_ref, wg_ref, wu_ref, wd_ref, o_ref):
    j = pl.program_id(2)
    h = h_ref[0]
    a = jnp.dot(h, wg_ref[...], preferred_element_type=F32)
    u = jnp.dot(h, wu_ref[...], preferred_element_type=F32)
    y = jnp.dot((_silu(a) * u).astype(BF16), wd_ref[...], preferred_element_type=F32)
    gate = mod_ref[0, 5:6, :]

    @pl.when(j == 0)
    def _():
        o_ref[0] = x_ref[0] + gate * y

    @pl.when(j > 0)
    def _():
        o_ref[0] = o_ref[0] + gate * y


def ffn_dense(h, x, mods, group_of_batch, wg, wu, wd, tm, tf=1408):
    b, t, d = x.shape
    f = wg.shape[1]
    tok = lambda w: pl.BlockSpec((1, tm, w), lambda i, s, j: (i, s, 0))
    if group_of_batch:
        mod_spec = pl.BlockSpec((1, N_MOD, d), lambda i, s, j: (i, 0, 0))
    else:
        mod_spec = pl.BlockSpec((1, N_MOD, d), lambda i, s, j: (0, 0, 0))
    return pl.pallas_call(
        _ffn_kernel,
        grid=(b, t // tm, f // tf),
        in_specs=[tok(d), tok(d), mod_spec,
                  pl.BlockSpec((d, tf), lambda i, s, j: (0, j)),
                  pl.BlockSpec((d, tf), lambda i, s, j: (0, j)),
                  pl.BlockSpec((tf, d), lambda i, s, j: (j, 0))],
        out_specs=tok(d),
        out_shape=jax.ShapeDtypeStruct((b, t, d), F32),
        compiler_params=_cparams("arbitrary", "arbitrary", "arbitrary"),
        name="ffn_dense",
    )(h, x, mods, wg, wu, wd)


def _router_kernel(h_ref, wr_ref, g_ref):
    nt = (((1,), (1,)), ((), ()))
    logits = lax.dot_general(wr_ref[...], h_ref[0].astype(F32), nt, precision=HIGHEST, preferred_element_type=F32)
    e_id = lax.broadcasted_iota(jnp.int32, logits.shape, 0)
    m1 = jnp.max(logits, axis=0, keepdims=True)
    i1 = jnp.min(jnp.where(logits == m1, e_id, N_EXPERTS), axis=0, keepdims=True)
    rest = jnp.where(e_id == i1, -jnp.inf, logits)
    m2 = jnp.max(rest, axis=0, keepdims=True)
    i2 = jnp.min(jnp.where(rest == m2, e_id, N_EXPERTS), axis=0, keepdims=True)
    w2 = 1.0 / (1.0 + jnp.exp(m1 - m2))
    w1 = 1.0 - w2
    g_ref[0] = jnp.where(e_id == i1, w1, 0.0) + jnp.where(e_id == i2, w2, 0.0)


def router_gates(h, w_router_t, tm):
    b, t, d = h.shape
    return pl.pallas_call(
        _router_kernel,
        grid=(b, t // tm),
        in_specs=[pl.BlockSpec((1, tm, d), lambda i, j: (i, j, 0)),
                  pl.BlockSpec(w_router_t.shape, lambda i, j: (0, 0))],
        out_specs=pl.BlockSpec((1, N_EXPERTS, tm), lambda i, j: (i, 0, j)),
        out_shape=jax.ShapeDtypeStruct((b, N_EXPERTS, t), F32),
        compiler_params=_cparams("arbitrary", "arbitrary"),
        name="router",
    )(h, w_router_t)


def _moe_kernel(h_ref, x_ref, g_ref, mod_ref, wg_ref, wu_ref, wd_ref, o_ref):
    e = pl.program_id(2)
    j = pl.program_id(3)
    h = h_ref[0]
    a = jnp.dot(h, wg_ref[0], preferred_element_type=F32)
    u = jnp.dot(h, wu_ref[0], preferred_element_type=F32)
    y = jnp.dot((_silu(a) * u).astype(BF16), wd_ref[0], preferred_element_type=F32)
    lane = lax.broadcasted_iota(jnp.int32, g_ref.shape[1:], 1)
    ge = jnp.sum(jnp.where(lane == e, g_ref[0], 0.0), axis=1, keepdims=True)
    y = (mod_ref[0, 5:6, :] * ge) * y

    @pl.when((e == 0) & (j == 0))
    def _():
        o_ref[0] = x_ref[0] + y

    @pl.when((e > 0) | (j > 0))
    def _():
        o_ref[0] = o_ref[0] + y


def moe_dense_gated(h, x, gates_t, mods, group_of_batch, wg, wu, wd, tm, tf=1408):
    b, t, d = x.shape
    f = wg.shape[2]
    tok = lambda w: pl.BlockSpec((1, tm, w), lambda i, s, e, j: (i, s, 0))
    if group_of_batch:
        mod_spec = pl.BlockSpec((1, N_MOD, d), lambda i, s, e, j: (i, 0, 0))
    else:
        mod_spec = pl.BlockSpec((1, N_MOD, d), lambda i, s, e, j: (0, 0, 0))
    return pl.pallas_call(
        _moe_kernel,
        grid=(b, t // tm, N_EXPERTS, f // tf),
        in_specs=[tok(d), tok(d), tok(HEAD_PAD), mod_spec,
                  pl.BlockSpec((1, d, tf), lambda i, s, e, j: (e, 0, j)),
                  pl.BlockSpec((1, d, tf), lambda i, s, e, j: (e, 0, j)),
                  pl.BlockSpec((1, tf, d), lambda i, s, e, j: (e, j, 0))],
        out_specs=tok(d),
        out_shape=jax.ShapeDtypeStruct((b, t, d), F32),
        compiler_params=_cparams("arbitrary", "arbitrary", "arbitrary", "arbitrary"),
        name="moe_dense_gated",
    )(h, x, gates_t, mods, wg, wu, wd)


def _final_kernel(x_ref, g_ref, o_ref):
    o_ref[0] = _rms(x_ref[0]) * g_ref[...]


def final_norm(x, gain, tm=1024):
    b, t, d = x.shape
    return pl.pallas_call(
        _final_kernel,
        grid=(b, t // tm),
        in_specs=[pl.BlockSpec((1, tm, d), lambda i, j: (i, j, 0)), pl.BlockSpec((1, d), lambda i, j: (0, 0))],
        out_specs=pl.BlockSpec((1, tm, d), lambda i, j: (i, j, 0)),
        out_shape=jax.ShapeDtypeStruct((b, t, d), F32),
        compiler_params=_cparams("arbitrary", "arbitrary"),
        name="final_norm",
    )(x, gain.reshape(1, d))


def _lower_bounds(lb_logits):
    p = jax.nn.softmax(lb_logits.astype(F32), axis=1)
    return jnp.cumsum(p, axis=1) - p[:, :1]


def _bidirectional_scan(hq, hf, hb, hi, lb_f, lb_b, s0_f, s0_b):
    o_f, s_f = hgrn_scan(hq, hf, hi, lb_f, s0_f)
    o_b, s_b = hgrn_scan(jnp.flip(hq, 1), jnp.flip(hb, 1), jnp.flip(hi, 1), lb_b, s0_b)
    return o_f, jnp.flip(o_b, 1), s_f, s_b


def _channel_mix(layer, h, x, mods, by_batch, ffn_w, moe_w, tm):
    j = layer // 2
    if layer % 2 == 0:
        wg, wu, wd = ffn_w
        return ffn_dense(h, x, mods, by_batch, wg[j].astype(BF16), wu[j].astype(BF16), wd[j].astype(BF16), tm)
    router, wg, wu, wd = moe_w
    gates = router_gates(h, router[j].T.astype(F32), min(tm, 1024))
    gates_t = jnp.pad(gates.transpose(0, 2, 1), ((0, 0), (0, 0), (0, HEAD_PAD - N_EXPERTS)))
    return moe_dense_gated(h, x, gates_t, mods, by_batch, wg[j].astype(BF16), wu[j].astype(BF16),
                           wd[j].astype(BF16), tm)


def kernel(x, c, ctx, c_ctx, w_ada, b_ada, w_in, q_norm_g, kv_norm_g, w_uq, w_ukv, hgrn_norm_g, lb_logits, w_out,
           ffn_w_gate, ffn_w_up, ffn_w_down, moe_router, moe_w_gate, moe_w_up, moe_w_down, final_g):
    bsz, n_lat, d = x.shape
    n_ctx = ctx.shape[1]
    depth = w_ada.shape[0]
    x = x.astype(F32)
    ctx = ctx.astype(F32)

    n_groups = 8
    cond = jnp.zeros((n_groups, d), F32).at[:bsz].set(c.astype(F32)).at[bsz].set(c_ctx.astype(F32))
    mods_all = ada_mods(cond, w_ada.astype(F32), b_ada.astype(F32)).reshape(depth, n_groups, N_MOD, d)
    lbs = _lower_bounds(lb_logits)
    tabs_lat = rope_tables(n_lat)
    tabs_ctx = identity_rope_tables(n_ctx)
    zero_state = jnp.zeros((bsz, HGRN_HEADS, HGRN_HEAD_V, HGRN_EXPAND), F32)
    ffn_w = (ffn_w_gate, ffn_w_up, ffn_w_down)
    moe_w = (moe_router, moe_w_gate, moe_w_up, moe_w_down)
    tm_lat = 512
    tm_ctx = n_ctx

    for layer in range(depth):
        need_ctx = layer < depth - 1
        mods_lat = mods_all[layer, :bsz]
        mods_ctx = mods_all[layer, bsz:bsz + 1]
        w_ext, wq_ext, wkv_ext = prep_in_weights(w_in[layer], w_uq[layer], w_ukv[layer])
        gq = q_norm_g[layer].astype(F32).reshape(1, -1)
        gkv = kv_norm_g[layer].astype(F32).reshape(1, -1)
        lb_f = lbs[0, layer].reshape(1, -1)
        lb_b = lbs[1, layer].reshape(1, -1)
        w_out_ext = prep_out_weights(w_out[layer])

        q_c, k_c, v_c, fo_c, hq_c, hf_c, hb_c, hi_c, hg_c = in_proj(
            ctx, mods_ctx, False, w_ext, gq, gkv, wq_ext, wkv_ext, tabs_ctx, tm_ctx)
        q_l, k_l, v_l, fo_l, hq_l, hf_l, hb_l, hi_l, hg_l = in_proj(
            x, mods_lat, True, w_ext, gq, gkv, wq_ext, wkv_ext, tabs_lat, tm_lat)

        att_l = attention(q_l, k_c, v_c, k_l, v_l)
        four_l = fourier_large(fo_l)
        oc_f, oc_b, sc_f, sc_b = _bidirectional_scan(hq_c, hf_c, hb_c, hi_c, lb_f, lb_b, zero_state, zero_state)
        ol_f, ol_b, _, _ = _bidirectional_scan(hq_l, hf_l, hb_l, hi_l, lb_f, lb_b, sc_f, sc_b)
        x, h_l = out_proj(x, att_l, four_l, ol_f, ol_b, hg_l, mods_lat, True, w_out_ext, hgrn_norm_g[layer], tm_lat)
        if need_ctx:
            att_c = attention(q_c, k_c, v_c)
            four_c = fourier_small(fo_c)
            ctx, h_c = out_proj(ctx, att_c, four_c, oc_f, oc_b, hg_c, mods_ctx, False, w_out_ext,
                                hgrn_norm_g[layer], tm_ctx)
            ctx = _channel_mix(layer, h_c, ctx, mods_ctx, False, ffn_w, moe_w, tm_ctx)
        x = _channel_mix(layer, h_l, x, mods_lat, True, ffn_w, moe_w, tm_lat)

    return final_norm(x, final_g.astype(F32))
```

```python
import functools

import numpy as np
import jax
import jax.numpy as jnp
from jax import lax
from jax.experimental import pallas as pl
from jax.experimental.pallas import tpu as pltpu

F32 = jnp.float32
BF16 = jnp.bfloat16
HIGHEST = lax.Precision.HIGHEST

EPS = 1e-6
N_MOD = 6
GRID_W = 64
ROPE_BASE = 10000.0

MLA_HEADS = 8
MLA_Q_RANK = 256
MLA_KV_RANK = 128
MLA_NOPE = 64
MLA_ROPE = 32
MLA_V = 64
MLA_SCALE = (MLA_NOPE + MLA_ROPE) ** -0.5
Q_SCALE = MLA_SCALE * float(np.log2(np.e))
HEAD_PAD = 128

FOURIER_GROUPS = 4
FOURIER_CH = 64
FOURIER_WIDTH = FOURIER_GROUPS * FOURIER_CH

HGRN_HEADS = 4
HGRN_EXPAND = 128
HGRN_HEAD_V = 64
HGRN_KEY_WIDTH = HGRN_HEADS * HGRN_EXPAND
HGRN_WIDTH = HGRN_HEADS * HGRN_HEAD_V
CHUNK = 64
SUB = 16
N_SUB = CHUNK // SUB

N_EXPERTS = 8
TOP_K = 2

IN_SIZES = (MLA_Q_RANK, MLA_KV_RANK, MLA_ROPE, FOURIER_WIDTH, HGRN_KEY_WIDTH, HGRN_KEY_WIDTH, HGRN_KEY_WIDTH,
            HGRN_WIDTH, HGRN_WIDTH)

V7X_VMEM_LIMIT = 56 * 1024 * 1024


def _cparams(*sem):
    return pltpu.CompilerParams(dimension_semantics=sem, vmem_limit_bytes=V7X_VMEM_LIMIT)


def _silu(x):
    return x * (1.0 / (1.0 + jnp.exp(-x)))


def _rms(x):
    return x * lax.rsqrt(jnp.mean(x * x, axis=-1, keepdims=True) + EPS)


def _ada_kernel(c_ref, w_ref, b_ref, o_ref):
    c = _silu(c_ref[...])
    o_ref[0] = jnp.dot(c, w_ref[0], precision=HIGHEST, preferred_element_type=F32) + b_ref[0]


def ada_mods(cond, w_ada, b_ada):
    depth, d, n = w_ada.shape
    g = cond.shape[0]
    tn = 1536
    return pl.pallas_call(
        _ada_kernel,
        grid=(depth, n // tn),
        in_specs=[pl.BlockSpec((g, d), lambda l, j: (0, 0)),
                  pl.BlockSpec((1, d, tn), lambda l, j: (l, 0, j)),
                  pl.BlockSpec((1, 1, tn), lambda l, j: (l, 0, j))],
        out_specs=pl.BlockSpec((1, g, tn), lambda l, j: (l, 0, j)),
        out_shape=jax.ShapeDtypeStruct((depth, g, n), F32),
        compiler_params=_cparams("arbitrary", "arbitrary"),
        name="ada_mods",
    )(cond, w_ada, b_ada.reshape(depth, 1, n))


C_DQ = 0
C_DKV = C_DQ + MLA_Q_RANK
C_KRA = C_DKV + MLA_KV_RANK
C_KRB = C_KRA + HEAD_PAD
C_FO = C_KRB + HEAD_PAD
C_HQ = C_FO + FOURIER_WIDTH
C_HF = C_HQ + HGRN_KEY_WIDTH
C_HB = C_HF + HGRN_KEY_WIDTH
C_HI = C_HB + HGRN_KEY_WIDTH
C_HG = C_HI + HGRN_WIDTH
C_END = C_HG + HGRN_WIDTH


def _rope_partner():
    i = np.arange(MLA_ROPE)
    half = (i % 16) // 8
    return np.where(half == 0, i + 8, i - 8)


def prep_in_weights(w_in, w_uq, w_ukv):
    d = w_in.shape[0]
    starts = np.cumsum((0,) + IN_SIZES)
    dq, dkv, kr, fo, hq, hf, hb, hi, hg = [w_in[:, starts[i]:starts[i + 1]] for i in range(len(IN_SIZES))]
    z64 = jnp.zeros((d, MLA_NOPE), w_in.dtype)
    z32 = jnp.zeros((d, HEAD_PAD - MLA_NOPE - MLA_ROPE), w_in.dtype)
    kra = jnp.concatenate([z64, kr, z32], axis=1)
    krb = jnp.concatenate([z64, kr[:, _rope_partner()], z32], axis=1)
    w_ext = jnp.concatenate([dq, dkv, kra, krb, fo, hq, hf, hb, hi, hg], axis=1).astype(BF16)

    r = w_uq.shape[0]
    wq = w_uq.reshape(r, MLA_HEADS, MLA_NOPE + MLA_ROPE)
    zq = jnp.zeros((r, MLA_HEADS, HEAD_PAD - MLA_NOPE - MLA_ROPE), w_uq.dtype)
    wqa = jnp.concatenate([wq, zq], axis=2).reshape(r, MLA_HEADS * HEAD_PAD)
    wq_sw = jnp.concatenate([jnp.zeros((r, MLA_HEADS, MLA_NOPE), w_uq.dtype),
                             wq[:, :, MLA_NOPE:][:, :, _rope_partner()], zq], axis=2)
    wqb = wq_sw.reshape(r, MLA_HEADS * HEAD_PAD)
    wq_ext = jnp.concatenate([wqa, wqb], axis=1).astype(BF16)

    rk = w_ukv.shape[0]
    wkv = w_ukv.reshape(rk, MLA_HEADS, MLA_NOPE + MLA_V)
    zk = jnp.zeros((rk, MLA_HEADS, HEAD_PAD - MLA_NOPE), w_ukv.dtype)
    wk = jnp.concatenate([wkv[:, :, :MLA_NOPE], zk], axis=2).reshape(rk, MLA_HEADS * HEAD_PAD)
    zv = jnp.zeros((rk, MLA_HEADS, HEAD_PAD - MLA_V), w_ukv.dtype)
    wv = jnp.concatenate([wkv[:, :, MLA_NOPE:], zv], axis=2).reshape(rk, MLA_HEADS * HEAD_PAD)
    wkv_ext = jnp.concatenate([wk, wv], axis=1).astype(BF16)
    return w_ext, wq_ext, wkv_ext


def rope_tables(n_tokens):
    rows = n_tokens // GRID_W
    row_ids = jnp.repeat(jnp.arange(rows, dtype=F32), GRID_W)
    col_ids = jnp.tile(jnp.arange(GRID_W, dtype=F32), rows)
    per_axis = MLA_ROPE // 2
    inv_freq = ROPE_BASE ** (-jnp.arange(0, per_axis, 2, dtype=F32) / per_axis)
    ang = jnp.concatenate([row_ids[:, None] * inv_freq, col_ids[:, None] * inv_freq], axis=-1)
    cos, sin = jnp.cos(ang), jnp.sin(ang)
    i = np.arange(MLA_ROPE)
    tab_idx = (i // 16) * 8 + (i % 8)
    sign = np.where((i % 16) // 8 == 0, -1.0, 1.0).astype(np.float32)
    c32 = cos[:, tab_idx]
    s32 = sin[:, tab_idx] * sign
    ones = jnp.ones((n_tokens, MLA_NOPE), F32)
    zeros = jnp.zeros((n_tokens, MLA_NOPE), F32)
    pad = jnp.zeros((n_tokens, HEAD_PAD - MLA_NOPE - MLA_ROPE), F32)
    ck = jnp.concatenate([ones, c32, pad], axis=1)
    sk = jnp.concatenate([zeros, s32, pad], axis=1)
    return ck * Q_SCALE, sk * Q_SCALE, ck, sk


def identity_rope_tables(n_tokens):
    lane = np.arange(HEAD_PAD)
    ck = jnp.asarray(np.broadcast_to((lane < MLA_NOPE + MLA_ROPE).astype(np.float32), (n_tokens, HEAD_PAD)))
    sk = jnp.zeros((n_tokens, HEAD_PAD), F32)
    return ck * Q_SCALE, sk, ck, sk


def _in_kernel(x_ref, mod_ref, w_ref, gq_ref, gkv_ref, wqt_ref, wk_ref, wvt_ref, cqt_ref, sqt_ref, ck_ref, sk_ref,
               qt_out, k_out, vt_out, fo_out, hq_out, hf_out, hb_out, hi_out, hg_out):
    x = x_ref[0]
    shift = mod_ref[0, 0:1, :]
    scale = mod_ref[0, 1:2, :]
    h = (_rms(x) * (1.0 + scale) + shift).astype(BF16)

    def proj(c0, c1):
        return jnp.dot(h, w_ref[:, c0:c1], preferred_element_type=F32)

    fo_out[0] = proj(C_FO, C_HQ).astype(fo_out.dtype)
    hq_out[0] = proj(C_HQ, C_HF)
    hf_out[0] = proj(C_HF, C_HB)
    hb_out[0] = proj(C_HB, C_HI)
    hi_out[0] = proj(C_HI, C_HG)
    hg_out[0] = proj(C_HG, C_END)

    hw = MLA_HEADS * HEAD_PAD
    nt = (((1,), (1,)), ((), ()))
    cq = (_rms(proj(C_DQ, C_DKV)) * gq_ref[...]).astype(BF16)
    qt2 = lax.dot_general(wqt_ref[...], cq, nt, preferred_element_type=F32)
    ckv = (_rms(proj(C_DKV, C_KRA)) * gkv_ref[...]).astype(BF16)
    kn = jnp.dot(ckv, wk_ref[...], preferred_element_type=F32)
    vt = lax.dot_general(wvt_ref[...], ckv, nt, preferred_element_type=F32)
    kr = proj(C_KRA, C_KRB) * ck_ref[...] + proj(C_KRB, C_FO) * sk_ref[...]
    cqt = cqt_ref[...]
    sqt = sqt_ref[...]
    for hd in range(MLA_HEADS):
        sl = slice(hd * HEAD_PAD, (hd + 1) * HEAD_PAD)
        sl2 = slice(hw + hd * HEAD_PAD, hw + (hd + 1) * HEAD_PAD)
        qt_out[0, sl, :] = (qt2[sl] * cqt + qt2[sl2] * sqt).astype(qt_out.dtype)
        k_out[0, :, sl] = (kn[:, sl] + kr).astype(k_out.dtype)
    row = lax.broadcasted_iota(jnp.int32, vt.shape, 0)
    vt_out[0] = jnp.where((row & (HEAD_PAD - 1)) == MLA_V, 1.0, vt).astype(vt_out.dtype)


def in_proj(x, mods, group_of_batch, w_ext, gq, gkv, wq_ext, wkv_ext, tabs, tm):
    b, t, d = x.shape
    hw = MLA_HEADS * HEAD_PAD
    cq, sq, ck, sk = tabs
    wqt = wq_ext.T
    wk = wkv_ext[:, :hw]
    wvt = wkv_ext[:, hw:].T
    grid = (b, t // tm)
    tok = lambda w: pl.BlockSpec((1, tm, w), lambda i, j: (i, j, 0))
    tok_t = pl.BlockSpec((1, hw, tm), lambda i, j: (i, 0, j))
    full2 = lambda a: pl.BlockSpec(a.shape, lambda i, j: (0, 0))
    tab = pl.BlockSpec((tm, HEAD_PAD), lambda i, j: (j, 0))
    tab_t = pl.BlockSpec((HEAD_PAD, tm), lambda i, j: (0, j))
    if group_of_batch:
        mod_spec = pl.BlockSpec((1, N_MOD, d), lambda i, j: (i, 0, 0))
    else:
        mod_spec = pl.BlockSpec((1, N_MOD, d), lambda i, j: (0, 0, 0))
    outs = [(FOURIER_WIDTH, F32), (HGRN_KEY_WIDTH, F32), (HGRN_KEY_WIDTH, F32),
            (HGRN_KEY_WIDTH, F32), (HGRN_WIDTH, F32), (HGRN_WIDTH, F32)]
    return pl.pallas_call(
        _in_kernel,
        grid=grid,
        in_specs=[tok(d), mod_spec, full2(w_ext), full2(gq), full2(gkv), full2(wqt), full2(wk), full2(wvt),
                  tab_t, tab_t, tab, tab],
        out_specs=[tok_t, tok(hw), tok_t] + [tok(w) for w, _ in outs],
        out_shape=[jax.ShapeDtypeStruct((b, hw, t), BF16), jax.ShapeDtypeStruct((b, t, hw), BF16),
                   jax.ShapeDtypeStruct((b, hw, t), BF16)]
                  + [jax.ShapeDtypeStruct((b, t, w), dt) for w, dt in outs],
        compiler_params=_cparams("arbitrary", "arbitrary"),
        name="in_proj",
    )(x, mods, w_ext, gq, gkv, wqt, wk, wvt, cq.T, sq.T, ck, sk)


def _attn_kernel(*refs, tk, n_lat):
    if n_lat:
        qt_ref, kc_ref, vtc_ref, kl_ref, vtl_ref, o_ref, m_scr, acc_scr, s_scr = refs
    else:
        qt_ref, kc_ref, vtc_ref, o_ref, m_scr, acc_scr = refs
    qt = qt_ref[0]

    def softmax_pv(st, vt):
        m_old = m_scr[...]
        m_new = jnp.maximum(m_old, jnp.max(st, axis=0, keepdims=True))
        pt = jnp.exp2(st - m_new).astype(BF16)
        acc_scr[...] = acc_scr[...] * jnp.exp2(m_old - m_new) + jnp.dot(vt, pt, preferred_element_type=F32)
        m_scr[...] = m_new

    m_scr[...] = jnp.full(m_scr.shape, -1e30, F32)
    acc_scr[...] = jnp.zeros(acc_scr.shape, F32)
    softmax_pv(jnp.dot(kc_ref[0], qt, preferred_element_type=F32), vtc_ref[0])
    if n_lat:
        n_blk = n_lat // tk

        def scores(i):
            start = pl.multiple_of(i * tk, tk)
            return jnp.dot(kl_ref[0, pl.ds(start, tk), :], qt, preferred_element_type=F32)

        def values(i):
            return vtl_ref[0, :, pl.ds(pl.multiple_of(i * tk, tk), tk)]

        s_scr[0] = scores(0)

        def body(p, carry):
            i = 2 * p
            s_scr[1] = scores(i + 1)
            softmax_pv(s_scr[0], values(i))
            s_scr[0] = scores(jnp.minimum(i + 2, n_blk - 1))
            softmax_pv(s_scr[1], values(i + 1))
            return carry
        lax.fori_loop(0, n_blk // 2, body, 0)
    acc = acc_scr[...]
    out_t = acc * (1.0 / acc[MLA_V:MLA_V + 1, :])
    o_ref[0] = out_t.T.astype(o_ref.dtype)


def attention(qt, k_ctx, vt_ctx, k_lat=None, vt_lat=None, tq=512, tk=1024):
    b, hw, t = qt.shape
    n_ctx = k_ctx.shape[1]
    n_lat = 0 if k_lat is None else k_lat.shape[1]
    tq = min(tq, t)
    scratch = [pltpu.VMEM((1, tq), F32), pltpu.VMEM((HEAD_PAD, tq), F32)]
    if n_lat:
        tk = min(tk, n_lat // 2)
        assert n_lat % (2 * tk) == 0
        scratch.append(pltpu.VMEM((2, tk, tq), F32))
    grid = (b, MLA_HEADS, t // tq)
    qspec = pl.BlockSpec((1, HEAD_PAD, tq), lambda i, h, j: (i, h, j))
    in_specs = [qspec,
                pl.BlockSpec((1, n_ctx, HEAD_PAD), lambda i, h, j: (i, 0, h)),
                pl.BlockSpec((1, HEAD_PAD, n_ctx), lambda i, h, j: (i, h, 0))]
    args = [qt, k_ctx, vt_ctx]
    if n_lat:
        in_specs += [pl.BlockSpec((1, n_lat, HEAD_PAD), lambda i, h, j: (i, 0, h)),
                     pl.BlockSpec((1, HEAD_PAD, n_lat), lambda i, h, j: (i, h, 0))]
        args += [k_lat, vt_lat]
    return pl.pallas_call(
        functools.partial(_attn_kernel, tk=tk, n_lat=n_lat),
        grid=grid,
        in_specs=in_specs,
        out_specs=pl.BlockSpec((1, tq, HEAD_PAD), lambda i, h, j: (i, j, h)),
        out_shape=jax.ShapeDtypeStruct((b, t, hw), BF16),
        scratch_shapes=scratch,
        compiler_params=_cparams("arbitrary", "arbitrary", "arbitrary"),
        name="attention",
    )(*args)


def _channel_dft_mats(norm):
    c = np.arange(FOURIER_CH)
    ang = 2.0 * np.pi * np.outer(c, c) / FOURIER_CH
    eye = np.eye(FOURIER_GROUPS)
    cc = np.kron(eye, np.cos(ang)) * norm
    sc = np.kron(eye, np.sin(ang)) * norm
    return np.concatenate([cc, sc], axis=0).astype(np.float32)


def _dft_small_kernel(u_ref, f_ref, cs_ref, o_ref):
    t = u_ref.shape[1]
    x = jnp.dot(f_ref[...], u_ref[0].astype(BF16), preferred_element_type=F32)
    w = FOURIER_WIDTH
    y = jnp.dot(x[:t].astype(BF16), cs_ref[:w], preferred_element_type=F32)
    y = y + jnp.dot(x[t:].astype(BF16), cs_ref[w:], preferred_element_type=F32)
    o_ref[0] = y.astype(o_ref.dtype)


def fourier_small(u):
    b, t, w = u.shape
    n = np.arange(t)
    ang = 2.0 * np.pi * (np.outer(n, n) % t) / t
    f = jnp.asarray(np.concatenate([np.cos(ang), -np.sin(ang)], axis=0), BF16)
    cs = jnp.asarray(_channel_dft_mats((t * FOURIER_CH) ** -0.5), BF16)
    return pl.pallas_call(
        _dft_small_kernel,
        grid=(b,),
        in_specs=[pl.BlockSpec((1, t, w), lambda i: (i, 0, 0)),
                  pl.BlockSpec(f.shape, lambda i: (0, 0)),
                  pl.BlockSpec(cs.shape, lambda i: (0, 0))],
        out_specs=pl.BlockSpec((1, t, w), lambda i: (i, 0, 0)),
        out_shape=jax.ShapeDtypeStruct((b, t, w), BF16),
        compiler_params=_cparams("arbitrary"),
        name="fourier_ctx",
    )(u, f, cs)


def _dft_stage1_kernel(x_ref, f_ref, o_ref):
    o_ref[0] = jnp.dot(f_ref[...], x_ref[0].astype(BF16), preferred_element_type=F32).astype(o_ref.dtype)


def _dft_stage2_kernel(z_ref, m_ref, cs_ref, o_ref, *, n2):
    w = FOURIER_WIDTH
    for j in range(z_ref.shape[1]):
        x = jnp.dot(m_ref[j], z_ref[0, j], preferred_element_type=F32)
        y = jnp.dot(x[:n2].astype(BF16), cs_ref[:w], preferred_element_type=F32)
        y = y + jnp.dot(x[n2:].astype(BF16), cs_ref[w:], preferred_element_type=F32)
        o_ref[0, j] = y.astype(o_ref.dtype)


def fourier_large(u, n1=128):
    bsz, t, w = u.shape
    n2 = t // n1
    a = np.arange(n1)
    ang1 = 2.0 * np.pi * np.outer(a, a) / n1
    f1 = np.empty((2 * n1, n1), np.float64)
    f1[0::2] = np.cos(ang1)
    f1[1::2] = -np.sin(ang1)
    f1 = jnp.asarray(f1, BF16)
    k1 = np.arange(n1)[:, None, None]
    k2 = np.arange(n2)[None, :, None]
    bb = np.arange(n2)[None, None, :]
    ang2 = 2.0 * np.pi * ((bb * (k1 + n1 * k2)) % t) / t
    mr, mi = np.cos(ang2), np.sin(ang2)
    m = np.concatenate([np.concatenate([mr, mi], axis=2), np.concatenate([-mi, mr], axis=2)], axis=1)
    m = jnp.asarray(m, BF16)
    cs = jnp.asarray(_channel_dft_mats((t * FOURIER_CH) ** -0.5), BF16)

    cols = n2 * w
    tc = 2048
    z = pl.pallas_call(
        _dft_stage1_kernel,
        grid=(bsz, cols // tc),
        in_specs=[pl.BlockSpec((1, n1, tc), lambda i, j: (i, 0, j)),
                  pl.BlockSpec(f1.shape, lambda i, j: (0, 0))],
        out_specs=pl.BlockSpec((1, 2 * n1, tc), lambda i, j: (i, 0, j)),
        out_shape=jax.ShapeDtypeStruct((bsz, 2 * n1, cols), BF16),
        compiler_params=_cparams("arbitrary", "arbitrary"),
        name="fourier_stage1",
    )(u.reshape(bsz, n1, cols), f1)
    z = z.reshape(bsz, n1, 2 * n2, w)
    kc = 16
    y = pl.pallas_call(
        functools.partial(_dft_stage2_kernel, n2=n2),
        grid=(bsz, n1 // kc),
        in_specs=[pl.BlockSpec((1, kc, 2 * n2, w), lambda i, j: (i, j, 0, 0)),
                  pl.BlockSpec((kc, 2 * n2, 2 * n2), lambda i, j: (j, 0, 0)),
                  pl.BlockSpec(cs.shape, lambda i, j: (0, 0))],
        out_specs=pl.BlockSpec((1, kc, n2, w), lambda i, j: (i, j, 0, 0)),
        out_shape=jax.ShapeDtypeStruct((bsz, n1, n2, w), BF16),
        compiler_params=_cparams("arbitrary", "arbitrary"),
        name="fourier_stage2",
    )(z, m, cs)
    return y.transpose(0, 2, 1, 3).reshape(bsz, t, w)


def _scan_consts(rev):
    r = np.arange(CHUNK)
    ltri = ((r[None, :] >= r[:, None]) if rev else (r[None, :] <= r[:, None])).astype(np.float32)
    e = np.zeros((SUB * HGRN_EXPAND, HEAD_PAD), np.float32)
    for s in range(SUB):
        e[s * HGRN_EXPAND:(s + 1) * HGRN_EXPAND, s:CHUNK:SUB] = 1.0
    return jnp.asarray(ltri), jnp.asarray(e, BF16)


def _scan_kernel(hq_ref, hf_ref, hi_ref, lb_ref, s0_ref, ltri_ref, e_ref, o_ref, sfin_ref, s_scr, *, n_chunks, rev):
    step = pl.program_id(1)

    @pl.when(step == 0)
    def _():
        s_scr[...] = s0_ref[0]

    lb = lb_ref[...]
    ltri = ltri_ref[...]
    row = lax.broadcasted_iota(jnp.int32, (CHUNK, CHUNK), 0)
    col = lax.broadcasted_iota(jnp.int32, (CHUNK, CHUNK), 1)
    sub_shift = SUB.bit_length() - 1
    same_sub = (col >> sub_shift) == (row >> sub_shift)
    if rev:
        diag_mask = same_sub & ((col & (SUB - 1)) >= (row & (SUB - 1)))
    else:
        diag_mask = same_sub & ((col & (SUB - 1)) <= (row & (SUB - 1)))
    sub_row = lax.broadcasted_iota(jnp.int32, (N_SUB, SUB, HGRN_EXPAND), 1)
    col_low = lax.broadcasted_iota(jnp.int32, (SUB, CHUNK), 1)
    end_row = 0 if rev else CHUNK - 1
    nt = (((1,), (1,)), ((), ()))
    tn = (((0,), (0,)), ((), ()))

    def chunk(c, carry):
        if rev:
            c = n_chunks - 1 - c
        r0 = pl.multiple_of(c * CHUNK, CHUNK)
        f = lb + (1.0 - lb) * (1.0 / (1.0 + jnp.exp(-hf_ref[0, pl.ds(r0, CHUNK), :])))
        logf = jnp.log(f)
        kk = 1.0 - f
        g = jnp.dot(ltri, logf, precision=HIGHEST, preferred_element_type=F32)
        q = _silu(hq_ref[0, pl.ds(r0, CHUNK), :])
        v = hi_ref[0, pl.ds(r0, CHUNK), :]
        outs = []
        for hd in range(HGRN_HEADS):
            ks = slice(hd * HGRN_EXPAND, (hd + 1) * HGRN_EXPAND)
            vs = slice(hd * HGRN_HEAD_V, (hd + 1) * HGRN_HEAD_V)
            gh, qh, kh, vh = g[:, ks], q[:, ks], kk[:, ks], v[:, vs]
            vb = vh.astype(BF16)
            st_prev = s_scr[hd]
            g_end = gh[end_row:end_row + 1, :]
            o = lax.dot_general((qh * jnp.exp(gh)).astype(BF16), st_prev.astype(BF16), nt,
                                preferred_element_type=F32)
            k_dec = (kh * jnp.exp(g_end - gh)).astype(BF16)
            upd = lax.dot_general(vb, k_dec, tn, preferred_element_type=F32)
            s_scr[hd] = jnp.exp(g_end) * st_prev + upd
            g3 = gh.reshape(N_SUB, SUB, HGRN_EXPAND)
            k3 = kh.reshape(N_SUB, SUB, HGRN_EXPAND)
            q3 = qh.reshape(N_SUB, SUB, HGRN_EXPAND)
            slots = []
            for j in range(SUB):
                gj = jnp.broadcast_to(g3[:, j:j + 1, :], g3.shape)
                kj = jnp.broadcast_to(k3[:, j:j + 1, :], k3.shape)
                seen = (sub_row <= j) if rev else (sub_row >= j)
                dec = jnp.exp(jnp.where(seen, g3 - gj, -jnp.inf))
                slots.append((q3 * kj * dec).reshape(CHUNK, HGRN_EXPAND).astype(BF16))
            pmat = jnp.concatenate(slots, axis=1)
            a = jnp.dot(pmat, e_ref[...], preferred_element_type=F32)[:, :CHUNK]
            a = jnp.where(diag_mask, a, 0.0)
            parts = []
            for i in range(N_SUB):
                rows = slice(i * SUB, (i + 1) * SUB)
                if (i == N_SUB - 1) if rev else (i == 0):
                    parts.append(a[rows])
                    continue
                ref_row = (i + 1) * SUB if rev else i * SUB - 1
                ref_g = gh[ref_row:ref_row + 1, :]
                qi = (qh[rows] * jnp.exp(gh[rows] - ref_g)).astype(BF16)
                kp = (kh * jnp.exp(jnp.minimum(ref_g - gh, 0.0))).astype(BF16)
                low = lax.dot_general(qi, kp, nt, preferred_element_type=F32)
                earlier = (col_low >= (i + 1) * SUB) if rev else (col_low < i * SUB)
                parts.append(a[rows] + jnp.where(earlier, low, 0.0))
            amat = jnp.concatenate(parts, axis=0)
            o = o + jnp.dot(amat.astype(BF16), vb, preferred_element_type=F32)
            outs.append(o)
        o_ref[0, pl.ds(r0, CHUNK), :] = jnp.concatenate(outs, axis=1)
        return carry

    lax.fori_loop(0, n_chunks, chunk, 0)

    @pl.when(step == pl.num_programs(1) - 1)
    def _():
        sfin_ref[0] = s_scr[...]


def hgrn_scan(hq, hf, hi, lb, s0, rev, tl=512):
    b, t, kw = hq.shape
    tl = min(tl, t)
    n_tiles = t // tl
    ltri, e = _scan_consts(rev)
    if rev:
        tok = lambda w: pl.BlockSpec((1, tl, w), lambda i, j: (i, n_tiles - 1 - j, 0))
    else:
        tok = lambda w: pl.BlockSpec((1, tl, w), lambda i, j: (i, j, 0))
    sspec = pl.BlockSpec((1, HGRN_HEADS, HGRN_HEAD_V, HGRN_EXPAND), lambda i, j: (i, 0, 0, 0))
    return pl.pallas_call(
        functools.partial(_scan_kernel, n_chunks=tl // CHUNK, rev=rev),
        grid=(b, n_tiles),
        in_specs=[tok(kw), tok(kw), tok(HGRN_WIDTH),
                  pl.BlockSpec(lb.shape, lambda i, j: (0, 0)), sspec,
                  pl.BlockSpec(ltri.shape, lambda i, j: (0, 0)),
                  pl.BlockSpec(e.shape, lambda i, j: (0, 0))],
        out_specs=[tok(HGRN_WIDTH), sspec],
        out_shape=[jax.ShapeDtypeStruct((b, t, HGRN_WIDTH), F32),
                   jax.ShapeDtypeStruct((b, HGRN_HEADS, HGRN_HEAD_V, HGRN_EXPAND), F32)],
        scratch_shapes=[pltpu.VMEM((HGRN_HEADS, HGRN_HEAD_V, HGRN_EXPAND), F32)],
        compiler_params=_cparams("arbitrary", "arbitrary"),
        name="hgrn_scan",
    )(hq, hf, hi, lb, s0, ltri, e)


def prep_out_weights(w_out):
    d = w_out.shape[1]
    wa = w_out[:MLA_HEADS * MLA_V].reshape(MLA_HEADS, MLA_V, d)
    wa = jnp.concatenate([wa, jnp.zeros((MLA_HEADS, HEAD_PAD - MLA_V, d), w_out.dtype)], axis=1)
    wa = wa.reshape(MLA_HEADS * HEAD_PAD, d)
    rest = w_out[MLA_HEADS * MLA_V:]
    return jnp.concatenate([wa, rest], axis=0).astype(BF16)


def _out_kernel(x_ref, att_ref, four_ref, of_ref, ob_ref, hg_ref, mod_ref, w_ref, gm_ref, gain_ref, x_out, h_out):
    hw = MLA_HEADS * HEAD_PAD
    o = of_ref[0] + ob_ref[0]
    ms = jnp.dot((o * o).astype(BF16), gm_ref[...], preferred_element_type=F32)
    rec = o * lax.rsqrt(ms + EPS) * gain_ref[...] * _silu(hg_ref[0])
    y = jnp.dot(att_ref[0], w_ref[:hw], preferred_element_type=F32)
    y = y + jnp.dot(four_ref[0], w_ref[hw:hw + FOURIER_WIDTH], preferred_element_type=F32)
    y = y + jnp.dot(rec.astype(BF16), w_ref[hw + FOURIER_WIDTH:], preferred_element_type=F32)
    x1 = x_ref[0] + mod_ref[0, 2:3, :] * y
    x_out[0] = x1
    h_out[0] = (_rms(x1) * (1.0 + mod_ref[0, 4:5, :]) + mod_ref[0, 3:4, :]).astype(h_out.dtype)


def out_proj(x, att, four, o_f, o_b, hg, mods, group_of_batch, w_out_ext, gain, tm):
    b, t, d = x.shape
    hw = MLA_HEADS * HEAD_PAD
    gm = jnp.asarray(np.kron(np.eye(HGRN_HEADS), np.full((HGRN_HEAD_V, HGRN_HEAD_V), 1.0 / HGRN_HEAD_V)), BF16)
    gain_t = jnp.tile(gain.astype(F32), HGRN_HEADS).reshape(1, HGRN_WIDTH)
    tok = lambda w: pl.BlockSpec((1, tm, w), lambda i, j: (i, j, 0))
    full2 = lambda a: pl.BlockSpec(a.shape, lambda i, j: (0, 0))
    if group_of_batch:
        mod_spec = pl.BlockSpec((1, N_MOD, d), lambda i, j: (i, 0, 0))
    else:
        mod_spec = pl.BlockSpec((1, N_MOD, d), lambda i, j: (0, 0, 0))
    return pl.pallas_call(
        _out_kernel,
        grid=(b, t // tm),
        in_specs=[tok(d), tok(hw), tok(FOURIER_WIDTH), tok(HGRN_WIDTH), tok(HGRN_WIDTH), tok(HGRN_WIDTH), mod_spec,
                  full2(w_out_ext), full2(gm), full2(gain_t)],
        out_specs=[tok(d), tok(d)],
        out_shape=[jax.ShapeDtypeStruct((b, t, d), F32), jax.ShapeDtypeStruct((b, t, d), BF16)],
        compiler_params=_cparams("arbitrary", "arbitrary"),
        name="out_proj",
    )(x, att, four, o_f, o_b, hg, mods, w_out_ext, gm, gain_t)


def _ffn_kernel(h_ref, x_ref, mod_ref, wg_ref, wu_ref, wd_ref, o_ref):
    j = pl.program_id(2)
    h = h_ref[0]
    a = jnp.dot(h, wg_ref[...], preferred_element_type=F32)
    u = jnp.dot(h, wu_ref[...], preferred_element_type=F32)
    y = jnp.dot((_silu(a) * u).astype(BF16), wd_ref[...], preferred_element_type=F32)
    gate = mod_ref[0, 5:6, :]

    @pl.when(j == 0)
    def _():
        o_ref[0] = x_ref[0] + gate * y

    @pl.when(j > 0)
    def _():
        o_ref[0] = o_ref[0] + gate * y


def ffn_dense(h, x, mods, group_of_batch, wg, wu, wd, tm, tf=1408):
    b, t, d = x.shape
    f = wg.shape[1]
    tok = lambda w: pl.BlockSpec((1, tm, w), lambda i, s, j: (i, s, 0))
    if group_of_batch:
        mod_spec = pl.BlockSpec((1, N_MOD, d), lambda i, s, j: (i, 0, 0))
    else:
        mod_spec = pl.BlockSpec((1, N_MOD, d), lambda i, s, j: (0, 0, 0))
    return pl.pallas_call(
        _ffn_kernel,
        grid=(b, t // tm, f // tf),
        in_specs=[tok(d), tok(d), mod_spec,
                  pl.BlockSpec((d, tf), lambda i, s, j: (0, j)),
                  pl.BlockSpec((d, tf), lambda i, s, j: (0, j)),
                  pl.BlockSpec((tf, d), lambda i, s, j: (j, 0))],
        out_specs=tok(d),
        out_shape=jax.ShapeDtypeStruct((b, t, d), F32),
        compiler_params=_cparams("arbitrary", "arbitrary", "arbitrary"),
        name="ffn_dense",
    )(h, x, mods, wg, wu, wd)


def _router_kernel(h_ref, wr_ref, g_ref):
    nt = (((1,), (1,)), ((), ()))
    logits = lax.dot_general(wr_ref[...], h_ref[0].astype(F32), nt, precision=HIGHEST, preferred_element_type=F32)
    e_id = lax.broadcasted_iota(jnp.int32, logits.shape, 0)
    m1 = jnp.max(logits, axis=0, keepdims=True)
    i1 = jnp.min(jnp.where(logits == m1, e_id, N_EXPERTS), axis=0, keepdims=True)
    rest = jnp.where(e_id == i1, -jnp.inf, logits)
    m2 = jnp.max(rest, axis=0, keepdims=True)
    i2 = jnp.min(jnp.where(rest == m2, e_id, N_EXPERTS), axis=0, keepdims=True)
    w2 = 1.0 / (1.0 + jnp.exp(m1 - m2))
    w1 = 1.0 - w2
    g_ref[0] = jnp.where(e_id == i1, w1, 0.0) + jnp.where(e_id == i2, w2, 0.0)


def router_gates(h, w_router_t, tm):
    b, t, d = h.shape
    return pl.pallas_call(
        _router_kernel,
        grid=(b, t // tm),
        in_specs=[pl.BlockSpec((1, tm, d), lambda i, j: (i, j, 0)),
                  pl.BlockSpec(w_router_t.shape, lambda i, j: (0, 0))],
        out_specs=pl.BlockSpec((1, N_EXPERTS, tm), lambda i, j: (i, 0, j)),
        out_shape=jax.ShapeDtypeStruct((b, N_EXPERTS, t), F32),
        compiler_params=_cparams("arbitrary", "arbitrary"),
        name="router",
    )(h, w_router_t)


MOE_CB = 128


def _moe_kernel(cnt_ref, h_ref, x_ref, gt_ref, g_ref, mod_ref, u_ref, wg_ref, wu_ref, wd_ref, o_ref,
                rank_scr, hc_scr, yc_scr, *, n_tiles):
    b, s, e, j = pl.program_id(0), pl.program_id(1), pl.program_id(2), pl.program_id(3)
    t = h_ref.shape[1]
    n_routed = cnt_ref[(b * n_tiles + s) * N_EXPERTS + e]
    n_blk = (n_routed + (MOE_CB - 1)) >> (MOE_CB.bit_length() - 1)
    sub = lax.broadcasted_iota(jnp.int32, (N_EXPERTS, t), 0)
    tn = (((0,), (0,)), ((), ()))

    @pl.when((e == 0) & (j == 0))
    def _():
        o_ref[0] = x_ref[0]
        sel = jnp.where(gt_ref[0] > 0.0, 1.0, 0.0).astype(BF16)
        rank_scr[...] = jnp.dot(sel, u_ref[...], preferred_element_type=F32)

    def one_hot(blk):
        gate = jnp.sum(jnp.where(sub == e, gt_ref[0], 0.0), axis=0, keepdims=True)
        rank = jnp.sum(jnp.where(sub == e, rank_scr[...], 0.0), axis=0, keepdims=True)
        rank = jnp.where(gate > 0.0, rank, -1.0)
        r = (lax.broadcasted_iota(jnp.int32, (MOE_CB, t), 0) + blk * MOE_CB).astype(F32)
        return jnp.where(r == rank, 1.0, 0.0).astype(BF16)

    @pl.when(j == 0)
    def _():
        def gather(blk, carry):
            r0 = pl.multiple_of(blk * MOE_CB, MOE_CB)
            hc = jnp.dot(one_hot(blk), h_ref[0], preferred_element_type=F32)
            hc_scr[pl.ds(r0, MOE_CB), :] = hc.astype(BF16)
            return carry
        lax.fori_loop(0, n_blk, gather, 0)

    def expert(blk, carry):
        r0 = pl.multiple_of(blk * MOE_CB, MOE_CB)
        hc = hc_scr[pl.ds(r0, MOE_CB), :]
        a = jnp.dot(hc, wg_ref[0], preferred_element_type=F32)
        u = jnp.dot(hc, wu_ref[0], preferred_element_type=F32)
        y = jnp.dot((_silu(a) * u).astype(BF16), wd_ref[0], preferred_element_type=F32)

        @pl.when(j == 0)
        def _():
            yc_scr[pl.ds(r0, MOE_CB), :] = y

        @pl.when(j > 0)
        def _():
            yc_scr[pl.ds(r0, MOE_CB), :] = yc_scr[pl.ds(r0, MOE_CB), :] + y
        return carry

    lax.fori_loop(0, n_blk, expert, 0)

    @pl.when(j == pl.num_programs(3) - 1)
    def _():
        lane = lax.broadcasted_iota(jnp.int32, g_ref.shape[1:], 1)
        ge = jnp.sum(jnp.where(lane == e, g_ref[0], 0.0), axis=1, keepdims=True)
        coef = mod_ref[0, 5:6, :] * ge

        def scatter(blk, carry):
            r0 = pl.multiple_of(blk * MOE_CB, MOE_CB)
            yc = yc_scr[pl.ds(r0, MOE_CB), :].astype(BF16)
            back = lax.dot_general(one_hot(blk), yc, tn, preferred_element_type=F32)
            o_ref[0] = o_ref[0] + coef * back
            return carry
        lax.fori_loop(0, n_blk, scatter, 0)


def moe_top2(h, x, gates, mods, group_of_batch, wg, wu, wd, tm, tf=1408):
    b, t, d = x.shape
    f = wg.shape[2]
    n_tiles = t // tm
    counts = jnp.sum((gates > 0.0).reshape(b, N_EXPERTS, n_tiles, tm), axis=-1, dtype=jnp.int32)
    counts = counts.transpose(0, 2, 1).reshape(-1)
    gates_lane = jnp.pad(gates.transpose(0, 2, 1), ((0, 0), (0, 0), (0, HEAD_PAD - N_EXPERTS)))
    r = np.arange(tm)
    upper = jnp.asarray(r[:, None] < r[None, :], BF16)
    tok = lambda w: pl.BlockSpec((1, tm, w), lambda i, s, e, j, c: (i, s, 0))
    if group_of_batch:
        mod_spec = pl.BlockSpec((1, N_MOD, d), lambda i, s, e, j, c: (i, 0, 0))
    else:
        mod_spec = pl.BlockSpec((1, N_MOD, d), lambda i, s, e, j, c: (0, 0, 0))
    grid_spec = pltpu.PrefetchScalarGridSpec(
        num_scalar_prefetch=1,
        grid=(b, n_tiles, N_EXPERTS, f // tf),
        in_specs=[tok(d), tok(d),
                  pl.BlockSpec((1, N_EXPERTS, tm), lambda i, s, e, j, c: (i, 0, s)),
                  tok(HEAD_PAD), mod_spec,
                  pl.BlockSpec((tm, tm), lambda i, s, e, j, c: (0, 0)),
                  pl.BlockSpec((1, d, tf), lambda i, s, e, j, c: (e, 0, j)),
                  pl.BlockSpec((1, d, tf), lambda i, s, e, j, c: (e, 0, j)),
                  pl.BlockSpec((1, tf, d), lambda i, s, e, j, c: (e, j, 0))],
        out_specs=tok(d),
        scratch_shapes=[pltpu.VMEM((N_EXPERTS, tm), F32), pltpu.VMEM((tm, d), BF16), pltpu.VMEM((tm, d), F32)],
    )
    return pl.pallas_call(
        functools.partial(_moe_kernel, n_tiles=n_tiles),
        grid_spec=grid_spec,
        out_shape=jax.ShapeDtypeStruct((b, t, d), F32),
        compiler_params=_cparams("arbitrary", "arbitrary", "arbitrary", "arbitrary"),
        name="moe_top2",
    )(counts, h, x, gates, gates_lane, mods, upper, wg, wu, wd)


def _final_kernel(x_ref, g_ref, o_ref):
    o_ref[0] = _rms(x_ref[0]) * g_ref[...]


def final_norm(x, gain, tm=1024):
    b, t, d = x.shape
    return pl.pallas_call(
        _final_kernel,
        grid=(b, t // tm),
        in_specs=[pl.BlockSpec((1, tm, d), lambda i, j: (i, j, 0)), pl.BlockSpec((1, d), lambda i, j: (0, 0))],
        out_specs=pl.BlockSpec((1, tm, d), lambda i, j: (i, j, 0)),
        out_shape=jax.ShapeDtypeStruct((b, t, d), F32),
        compiler_params=_cparams("arbitrary", "arbitrary"),
        name="final_norm",
    )(x, gain.reshape(1, d))


def _lower_bounds(lb_logits):
    p = jax.nn.softmax(lb_logits.astype(F32), axis=1)
    return jnp.cumsum(p, axis=1) - p[:, :1]


def _bidirectional_scan(hq, hf, hb, hi, lb_f, lb_b, s0_f, s0_b):
    o_f, s_f = hgrn_scan(hq, hf, hi, lb_f, s0_f, False)
    o_b, s_b = hgrn_scan(hq, hb, hi, lb_b, s0_b, True)
    return o_f, o_b, s_f, s_b


def _channel_mix(layer, h, x, mods, by_batch, ffn_w, moe_w, tm):
    j = layer // 2
    if layer % 2 == 0:
        wg, wu, wd = ffn_w
        return ffn_dense(h, x, mods, by_batch, wg[j].astype(BF16), wu[j].astype(BF16), wd[j].astype(BF16), tm)
    router, wg, wu, wd = moe_w
    gates = router_gates(h, router[j].T.astype(F32), tm)
    return moe_top2(h, x, gates, mods, by_batch, wg[j].astype(BF16), wu[j].astype(BF16), wd[j].astype(BF16),
                    min(x.shape[1], 1024))


def kernel(x, c, ctx, c_ctx, w_ada, b_ada, w_in, q_norm_g, kv_norm_g, w_uq, w_ukv, hgrn_norm_g, lb_logits, w_out,
           ffn_w_gate, ffn_w_up, ffn_w_down, moe_router, moe_w_gate, moe_w_up, moe_w_down, final_g):
    bsz, n_lat, d = x.shape
    n_ctx = ctx.shape[1]
    depth = w_ada.shape[0]
    x = x.astype(F32)
    ctx = ctx.astype(F32)

    n_groups = 8
    cond = jnp.zeros((n_groups, d), F32).at[:bsz].set(c.astype(F32)).at[bsz].set(c_ctx.astype(F32))
    mods_all = ada_mods(cond, w_ada.astype(F32), b_ada.astype(F32)).reshape(depth, n_groups, N_MOD, d)
    lbs = _lower_bounds(lb_logits)
    tabs_lat = rope_tables(n_lat)
    tabs_ctx = identity_rope_tables(n_ctx)
    zero_state = jnp.zeros((bsz, HGRN_HEADS, HGRN_HEAD_V, HGRN_EXPAND), F32)
    ffn_w = (ffn_w_gate, ffn_w_up, ffn_w_down)
    moe_w = (moe_router, moe_w_gate, moe_w_up, moe_w_down)
    tm_lat = 512
    tm_ctx = n_ctx

    for layer in range(depth):
        need_ctx = layer < depth - 1
        mods_lat = mods_all[layer, :bsz]
        mods_ctx = mods_all[layer, bsz:bsz + 1]
        w_ext, wq_ext, wkv_ext = prep_in_weights(w_in[layer], w_uq[layer], w_ukv[layer])
        gq = q_norm_g[layer].astype(F32).reshape(1, -1)
        gkv = kv_norm_g[layer].astype(F32).reshape(1, -1)
        lb_f = lbs[0, layer].reshape(1, -1)
        lb_b = lbs[1, layer].reshape(1, -1)
        w_out_ext = prep_out_weights(w_out[layer])

        q_c, k_c, v_c, fo_c, hq_c, hf_c, hb_c, hi_c, hg_c = in_proj(
            ctx, mods_ctx, False, w_ext, gq, gkv, wq_ext, wkv_ext, tabs_ctx, tm_ctx)
        q_l, k_l, v_l, fo_l, hq_l, hf_l, hb_l, hi_l, hg_l = in_proj(
            x, mods_lat, True, w_ext, gq, gkv, wq_ext, wkv_ext, tabs_lat, tm_lat)

        att_l = attention(q_l, k_c, v_c, k_l, v_l)
        four_l = fourier_large(fo_l)
        oc_f, oc_b, sc_f, sc_b = _bidirectional_scan(hq_c, hf_c, hb_c, hi_c, lb_f, lb_b, zero_state, zero_state)
        ol_f, ol_b, _, _ = _bidirectional_scan(hq_l, hf_l, hb_l, hi_l, lb_f, lb_b, sc_f, sc_b)
        x, h_l = out_proj(x, att_l, four_l, ol_f, ol_b, hg_l, mods_lat, True, w_out_ext, hgrn_norm_g[layer], tm_lat)
        if need_ctx:
            att_c = attention(q_c, k_c, v_c)
            four_c = fourier_small(fo_c)
            ctx, h_c = out_proj(ctx, att_c, four_c, oc_f, oc_b, hg_c, mods_ctx, False, w_out_ext,
                                hgrn_norm_g[layer], tm_ctx)
            ctx = _channel_mix(layer, h_c, ctx, mods_ctx, False, ffn_w, moe_w, tm_ctx)
        x = _channel_mix(layer, h_l, x, mods_lat, True, ffn_w, moe_w, tm_lat)

    return final_norm(x, final_g.astype(F32))
```

```python
import functools

import numpy as np
import jax
import jax.numpy as jnp
from jax import lax
from jax.experimental import pallas as pl
from jax.experimental.pallas import tpu as pltpu

F32 = jnp.float32
BF16 = jnp.bfloat16
HIGHEST = lax.Precision.HIGHEST

EPS = 1e-6
N_MOD = 6
GRID_W = 64
ROPE_BASE = 10000.0

MLA_HEADS = 8
MLA_Q_RANK = 256
MLA_KV_RANK = 128
MLA_NOPE = 64
MLA_ROPE = 32
MLA_V = 64
MLA_SCALE = (MLA_NOPE + MLA_ROPE) ** -0.5
Q_SCALE = MLA_SCALE * float(np.log2(np.e))
HEAD_PAD = 128
PV_ROWS = 80

FOURIER_GROUPS = 4
FOURIER_CH = 64
FOURIER_WIDTH = FOURIER_GROUPS * FOURIER_CH

HGRN_HEADS = 4
HGRN_EXPAND = 128
HGRN_HEAD_V = 64
HGRN_KEY_WIDTH = HGRN_HEADS * HGRN_EXPAND
HGRN_WIDTH = HGRN_HEADS * HGRN_HEAD_V
CHUNK = 64
SUB = 16
N_SUB = CHUNK // SUB

N_EXPERTS = 8
TOP_K = 2

IN_SIZES = (MLA_Q_RANK, MLA_KV_RANK, MLA_ROPE, FOURIER_WIDTH, HGRN_KEY_WIDTH, HGRN_KEY_WIDTH, HGRN_KEY_WIDTH,
            HGRN_WIDTH, HGRN_WIDTH)

V7X_VMEM_LIMIT = 56 * 1024 * 1024


def _cparams(*sem):
    return pltpu.CompilerParams(dimension_semantics=sem, vmem_limit_bytes=V7X_VMEM_LIMIT)


def _silu(x):
    return x * (1.0 / (1.0 + jnp.exp(-x)))


def _rms(x):
    return x * lax.rsqrt(jnp.mean(x * x, axis=-1, keepdims=True) + EPS)


def _ada_kernel(c_ref, w_ref, b_ref, o_ref):
    c = _silu(c_ref[...])
    o_ref[0] = jnp.dot(c, w_ref[0], precision=HIGHEST, preferred_element_type=F32) + b_ref[0]


def ada_mods(cond, w_ada, b_ada):
    depth, d, n = w_ada.shape
    g = cond.shape[0]
    tn = 1536
    return pl.pallas_call(
        _ada_kernel,
        grid=(depth, n // tn),
        in_specs=[pl.BlockSpec((g, d), lambda l, j: (0, 0)),
                  pl.BlockSpec((1, d, tn), lambda l, j: (l, 0, j)),
                  pl.BlockSpec((1, 1, tn), lambda l, j: (l, 0, j))],
        out_specs=pl.BlockSpec((1, g, tn), lambda l, j: (l, 0, j)),
        out_shape=jax.ShapeDtypeStruct((depth, g, n), F32),
        compiler_params=_cparams("arbitrary", "arbitrary"),
        name="ada_mods",
    )(cond, w_ada, b_ada.reshape(depth, 1, n))


C_DQ = 0
C_DKV = C_DQ + MLA_Q_RANK
C_KRA = C_DKV + MLA_KV_RANK
C_KRB = C_KRA + HEAD_PAD
C_FO = C_KRB + HEAD_PAD
C_HQ = C_FO + FOURIER_WIDTH
C_HF = C_HQ + HGRN_KEY_WIDTH
C_HB = C_HF + HGRN_KEY_WIDTH
C_HI = C_HB + HGRN_KEY_WIDTH
C_HG = C_HI + HGRN_WIDTH
C_END = C_HG + HGRN_WIDTH


def _rope_partner():
    i = np.arange(MLA_ROPE)
    half = (i % 16) // 8
    return np.where(half == 0, i + 8, i - 8)


def prep_in_weights(w_in, w_uq, w_ukv):
    d = w_in.shape[0]
    starts = np.cumsum((0,) + IN_SIZES)
    dq, dkv, kr, fo, hq, hf, hb, hi, hg = [w_in[:, starts[i]:starts[i + 1]] for i in range(len(IN_SIZES))]
    z64 = jnp.zeros((d, MLA_NOPE), w_in.dtype)
    z32 = jnp.zeros((d, HEAD_PAD - MLA_NOPE - MLA_ROPE), w_in.dtype)
    kra = jnp.concatenate([z64, kr, z32], axis=1)
    krb = jnp.concatenate([z64, kr[:, _rope_partner()], z32], axis=1)
    w_ext = jnp.concatenate([dq, dkv, kra, krb, fo, hq, hf, hb, hi, hg], axis=1).astype(BF16)

    r = w_uq.shape[0]
    wq = w_uq.reshape(r, MLA_HEADS, MLA_NOPE + MLA_ROPE)
    zq = jnp.zeros((r, MLA_HEADS, HEAD_PAD - MLA_NOPE - MLA_ROPE), w_uq.dtype)
    wqa = jnp.concatenate([wq, zq], axis=2).reshape(r, MLA_HEADS * HEAD_PAD)
    wq_sw = jnp.concatenate([jnp.zeros((r, MLA_HEADS, MLA_NOPE), w_uq.dtype),
                             wq[:, :, MLA_NOPE:][:, :, _rope_partner()], zq], axis=2)
    wqb = wq_sw.reshape(r, MLA_HEADS * HEAD_PAD)
    wq_ext = jnp.concatenate([wqa, wqb], axis=1).astype(BF16)

    rk = w_ukv.shape[0]
    wkv = w_ukv.reshape(rk, MLA_HEADS, MLA_NOPE + MLA_V)
    zk = jnp.zeros((rk, MLA_HEADS, HEAD_PAD - MLA_NOPE), w_ukv.dtype)
    wk = jnp.concatenate([wkv[:, :, :MLA_NOPE], zk], axis=2).reshape(rk, MLA_HEADS * HEAD_PAD)
    zv = jnp.zeros((rk, MLA_HEADS, HEAD_PAD - MLA_V), w_ukv.dtype)
    wv = jnp.concatenate([wkv[:, :, MLA_NOPE:], zv], axis=2).reshape(rk, MLA_HEADS * HEAD_PAD)
    wkv_ext = jnp.concatenate([wk, wv], axis=1).astype(BF16)
    return w_ext, wq_ext, wkv_ext


def rope_tables(n_tokens):
    rows = n_tokens // GRID_W
    row_ids = jnp.repeat(jnp.arange(rows, dtype=F32), GRID_W)
    col_ids = jnp.tile(jnp.arange(GRID_W, dtype=F32), rows)
    per_axis = MLA_ROPE // 2
    inv_freq = ROPE_BASE ** (-jnp.arange(0, per_axis, 2, dtype=F32) / per_axis)
    ang = jnp.concatenate([row_ids[:, None] * inv_freq, col_ids[:, None] * inv_freq], axis=-1)
    cos, sin = jnp.cos(ang), jnp.sin(ang)
    i = np.arange(MLA_ROPE)
    tab_idx = (i // 16) * 8 + (i % 8)
    sign = np.where((i % 16) // 8 == 0, -1.0, 1.0).astype(np.float32)
    c32 = cos[:, tab_idx]
    s32 = sin[:, tab_idx] * sign
    ones = jnp.ones((n_tokens, MLA_NOPE), F32)
    zeros = jnp.zeros((n_tokens, MLA_NOPE), F32)
    pad = jnp.zeros((n_tokens, HEAD_PAD - MLA_NOPE - MLA_ROPE), F32)
    ck = jnp.concatenate([ones, c32, pad], axis=1)
    sk = jnp.concatenate([zeros, s32, pad], axis=1)
    return ck * Q_SCALE, sk * Q_SCALE, ck, sk


def identity_rope_tables(n_tokens):
    lane = np.arange(HEAD_PAD)
    ck = jnp.asarray(np.broadcast_to((lane < MLA_NOPE + MLA_ROPE).astype(np.float32), (n_tokens, HEAD_PAD)))
    sk = jnp.zeros((n_tokens, HEAD_PAD), F32)
    return ck * Q_SCALE, sk, ck, sk


def _in_kernel(x_ref, mod_ref, w_ref, gq_ref, gkv_ref, wqt_ref, wk_ref, wvt_ref, cqt_ref, sqt_ref, ck_ref, sk_ref,
               qt_out, k_out, vt_out, fo_out, hq_out, hf_out, hb_out, hi_out, hg_out):
    x = x_ref[0]
    shift = mod_ref[0, 0:1, :]
    scale = mod_ref[0, 1:2, :]
    h = (_rms(x) * (1.0 + scale) + shift).astype(BF16)

    def proj(c0, c1):
        return jnp.dot(h, w_ref[:, c0:c1], preferred_element_type=F32)

    fo_out[0] = proj(C_FO, C_HQ).astype(fo_out.dtype)
    hq_out[0] = proj(C_HQ, C_HF)
    hf_out[0] = proj(C_HF, C_HB)
    hb_out[0] = proj(C_HB, C_HI)
    hi_out[0] = proj(C_HI, C_HG)
    hg_out[0] = proj(C_HG, C_END)

    hw = MLA_HEADS * HEAD_PAD
    nt = (((1,), (1,)), ((), ()))
    cq = (_rms(proj(C_DQ, C_DKV)) * gq_ref[...]).astype(BF16)
    qt2 = lax.dot_general(wqt_ref[...], cq, nt, preferred_element_type=F32)
    ckv = (_rms(proj(C_DKV, C_KRA)) * gkv_ref[...]).astype(BF16)
    kn = jnp.dot(ckv, wk_ref[...], preferred_element_type=F32)
    vt = lax.dot_general(wvt_ref[...], ckv, nt, preferred_element_type=F32)
    kr = proj(C_KRA, C_KRB) * ck_ref[...] + proj(C_KRB, C_FO) * sk_ref[...]
    cqt = cqt_ref[...]
    sqt = sqt_ref[...]
    for hd in range(MLA_HEADS):
        sl = slice(hd * HEAD_PAD, (hd + 1) * HEAD_PAD)
        sl2 = slice(hw + hd * HEAD_PAD, hw + (hd + 1) * HEAD_PAD)
        qt_out[0, sl, :] = (qt2[sl] * cqt + qt2[sl2] * sqt).astype(qt_out.dtype)
        k_out[0, :, sl] = (kn[:, sl] + kr).astype(k_out.dtype)
    row = lax.broadcasted_iota(jnp.int32, vt.shape, 0)
    vt_out[0] = jnp.where((row & (HEAD_PAD - 1)) == MLA_V, 1.0, vt).astype(vt_out.dtype)


def in_proj(x, mods, group_of_batch, w_ext, gq, gkv, wq_ext, wkv_ext, tabs, tm):
    b, t, d = x.shape
    hw = MLA_HEADS * HEAD_PAD
    cq, sq, ck, sk = tabs
    wqt = wq_ext.T
    wk = wkv_ext[:, :hw]
    wvt = wkv_ext[:, hw:].T
    grid = (b, t // tm)
    tok = lambda w: pl.BlockSpec((1, tm, w), lambda i, j: (i, j, 0))
    tok_t = pl.BlockSpec((1, hw, tm), lambda i, j: (i, 0, j))
    full2 = lambda a: pl.BlockSpec(a.shape, lambda i, j: (0, 0))
    tab = pl.BlockSpec((tm, HEAD_PAD), lambda i, j: (j, 0))
    tab_t = pl.BlockSpec((HEAD_PAD, tm), lambda i, j: (0, j))
    if group_of_batch:
        mod_spec = pl.BlockSpec((1, N_MOD, d), lambda i, j: (i, 0, 0))
    else:
        mod_spec = pl.BlockSpec((1, N_MOD, d), lambda i, j: (0, 0, 0))
    outs = [(FOURIER_WIDTH, F32), (HGRN_KEY_WIDTH, F32), (HGRN_KEY_WIDTH, F32),
            (HGRN_KEY_WIDTH, F32), (HGRN_WIDTH, F32), (HGRN_WIDTH, F32)]
    return pl.pallas_call(
        _in_kernel,
        grid=grid,
        in_specs=[tok(d), mod_spec, full2(w_ext), full2(gq), full2(gkv), full2(wqt), full2(wk), full2(wvt),
                  tab_t, tab_t, tab, tab],
        out_specs=[tok_t, tok(hw), tok_t] + [tok(w) for w, _ in outs],
        out_shape=[jax.ShapeDtypeStruct((b, hw, t), BF16), jax.ShapeDtypeStruct((b, t, hw), BF16),
                   jax.ShapeDtypeStruct((b, hw, t), BF16)]
                  + [jax.ShapeDtypeStruct((b, t, w), dt) for w, dt in outs],
        compiler_params=_cparams("arbitrary", "arbitrary"),
        name="in_proj",
    )(x, mods, w_ext, gq, gkv, wqt, wk, wvt, cq.T, sq.T, ck, sk)


def _attn_kernel(*refs, tk, n_lat):
    if n_lat:
        qt_ref, kc_ref, vtc_ref, kl_ref, vtl_ref, o_ref, m_scr, acc_scr, s_scr = refs
    else:
        qt_ref, kc_ref, vtc_ref, o_ref, m_scr, acc_scr = refs
    qt = qt_ref[0]

    def softmax_pv(st, vt):
        m_old = m_scr[...]
        m_new = jnp.maximum(m_old, jnp.max(st, axis=0, keepdims=True))
        pt = jnp.exp2(st - m_new).astype(BF16)
        acc_scr[...] = acc_scr[...] * jnp.exp2(m_old - m_new) + jnp.dot(vt, pt, preferred_element_type=F32)
        m_scr[...] = m_new

    m_scr[...] = jnp.full(m_scr.shape, -1e30, F32)
    acc_scr[...] = jnp.zeros(acc_scr.shape, F32)
    softmax_pv(jnp.dot(kc_ref[0], qt, preferred_element_type=F32), vtc_ref[0, :PV_ROWS, :])
    if n_lat:
        n_blk = n_lat // tk

        def scores(i):
            start = pl.multiple_of(i * tk, tk)
            return jnp.dot(kl_ref[0, pl.ds(start, tk), :], qt, preferred_element_type=F32)

        def values(i):
            return vtl_ref[0, :PV_ROWS, pl.ds(pl.multiple_of(i * tk, tk), tk)]

        s_scr[0] = scores(0)

        def body(p, carry):
            i = 2 * p
            s_scr[1] = scores(i + 1)
            softmax_pv(s_scr[0], values(i))
            s_scr[0] = scores(jnp.minimum(i + 2, n_blk - 1))
            softmax_pv(s_scr[1], values(i + 1))
            return carry
        lax.fori_loop(0, n_blk // 2, body, 0)
    acc = acc_scr[...]
    out_t = acc * (1.0 / acc[MLA_V:MLA_V + 1, :])
    out_t = jnp.concatenate([out_t, jnp.zeros((HEAD_PAD - PV_ROWS, out_t.shape[1]), F32)], axis=0)
    o_ref[0] = out_t.T.astype(o_ref.dtype)


def attention(qt, k_ctx, vt_ctx, k_lat=None, vt_lat=None, tq=1024, tk=1024):
    b, hw, t = qt.shape
    n_ctx = k_ctx.shape[1]
    n_lat = 0 if k_lat is None else k_lat.shape[1]
    tq = min(tq, t)
    scratch = [pltpu.VMEM((1, tq), F32), pltpu.VMEM((PV_ROWS, tq), F32)]
    if n_lat:
        tk = min(tk, n_lat // 2)
        assert n_lat % (2 * tk) == 0
        scratch.append(pltpu.VMEM((2, tk, tq), F32))
    grid = (b, MLA_HEADS, t // tq)
    qspec = pl.BlockSpec((1, HEAD_PAD, tq), lambda i, h, j: (i, h, j))
    in_specs = [qspec,
                pl.BlockSpec((1, n_ctx, HEAD_PAD), lambda i, h, j: (i, 0, h)),
                pl.BlockSpec((1, HEAD_PAD, n_ctx), lambda i, h, j: (i, h, 0))]
    args = [qt, k_ctx, vt_ctx]
    if n_lat:
        in_specs += [pl.BlockSpec((1, n_lat, HEAD_PAD), lambda i, h, j: (i, 0, h)),
                     pl.BlockSpec((1, HEAD_PAD, n_lat), lambda i, h, j: (i, h, 0))]
        args += [k_lat, vt_lat]
    return pl.pallas_call(
        functools.partial(_attn_kernel, tk=tk, n_lat=n_lat),
        grid=grid,
        in_specs=in_specs,
        out_specs=pl.BlockSpec((1, tq, HEAD_PAD), lambda i, h, j: (i, j, h)),
        out_shape=jax.ShapeDtypeStruct((b, t, hw), BF16),
        scratch_shapes=scratch,
        compiler_params=_cparams("arbitrary", "arbitrary", "arbitrary"),
        name="attention",
    )(*args)


def _channel_dft_mats(norm):
    c = np.arange(FOURIER_CH)
    ang = 2.0 * np.pi * np.outer(c, c) / FOURIER_CH
    eye = np.eye(FOURIER_GROUPS)
    cc = np.kron(eye, np.cos(ang)) * norm
    sc = np.kron(eye, np.sin(ang)) * norm
    return np.concatenate([cc, sc], axis=0).astype(np.float32)


def _dft_small_kernel(u_ref, f_ref, cs_ref, o_ref):
    t = u_ref.shape[1]
    x = jnp.dot(f_ref[...], u_ref[0].astype(BF16), preferred_element_type=F32)
    w = FOURIER_WIDTH
    y = jnp.dot(x[:t].astype(BF16), cs_ref[:w], preferred_element_type=F32)
    y = y + jnp.dot(x[t:].astype(BF16), cs_ref[w:], preferred_element_type=F32)
    o_ref[0] = y.astype(o_ref.dtype)


def fourier_small(u):
    b, t, w = u.shape
    n = np.arange(t)
    ang = 2.0 * np.pi * (np.outer(n, n) % t) / t
    f = jnp.asarray(np.concatenate([np.cos(ang), -np.sin(ang)], axis=0), BF16)
    cs = jnp.asarray(_channel_dft_mats((t * FOURIER_CH) ** -0.5), BF16)
    return pl.pallas_call(
        _dft_small_kernel,
        grid=(b,),
        in_specs=[pl.BlockSpec((1, t, w), lambda i: (i, 0, 0)),
                  pl.BlockSpec(f.shape, lambda i: (0, 0)),
                  pl.BlockSpec(cs.shape, lambda i: (0, 0))],
        out_specs=pl.BlockSpec((1, t, w), lambda i: (i, 0, 0)),
        out_shape=jax.ShapeDtypeStruct((b, t, w), BF16),
        compiler_params=_cparams("arbitrary"),
        name="fourier_ctx",
    )(u, f, cs)


def _dft_stage1_kernel(x_ref, f_ref, o_ref):
    o_ref[0] = jnp.dot(f_ref[...], x_ref[0].astype(BF16), preferred_element_type=F32).astype(o_ref.dtype)


def _dft_stage2_kernel(z_ref, m_ref, cs_ref, o_ref, *, n2):
    w = FOURIER_WIDTH
    for j in range(z_ref.shape[1]):
        x = jnp.dot(m_ref[j], z_ref[0, j], preferred_element_type=F32)
        y = jnp.dot(x[:n2].astype(BF16), cs_ref[:w], preferred_element_type=F32)
        y = y + jnp.dot(x[n2:].astype(BF16), cs_ref[w:], preferred_element_type=F32)
        o_ref[0, j] = y.astype(o_ref.dtype)


def fourier_large(u, n1=128):
    bsz, t, w = u.shape
    n2 = t // n1
    a = np.arange(n1)
    ang1 = 2.0 * np.pi * np.outer(a, a) / n1
    f1 = np.empty((2 * n1, n1), np.float64)
    f1[0::2] = np.cos(ang1)
    f1[1::2] = -np.sin(ang1)
    f1 = jnp.asarray(f1, BF16)
    k1 = np.arange(n1)[:, None, None]
    k2 = np.arange(n2)[None, :, None]
    bb = np.arange(n2)[None, None, :]
    ang2 = 2.0 * np.pi * ((bb * (k1 + n1 * k2)) % t) / t
    mr, mi = np.cos(ang2), np.sin(ang2)
    m = np.concatenate([np.concatenate([mr, mi], axis=2), np.concatenate([-mi, mr], axis=2)], axis=1)
    m = jnp.asarray(m, BF16)
    cs = jnp.asarray(_channel_dft_mats((t * FOURIER_CH) ** -0.5), BF16)

    cols = n2 * w
    tc = 2048
    z = pl.pallas_call(
        _dft_stage1_kernel,
        grid=(bsz, cols // tc),
        in_specs=[pl.BlockSpec((1, n1, tc), lambda i, j: (i, 0, j)),
                  pl.BlockSpec(f1.shape, lambda i, j: (0, 0))],
        out_specs=pl.BlockSpec((1, 2 * n1, tc), lambda i, j: (i, 0, j)),
        out_shape=jax.ShapeDtypeStruct((bsz, 2 * n1, cols), BF16),
        compiler_params=_cparams("arbitrary", "arbitrary"),
        name="fourier_stage1",
    )(u.reshape(bsz, n1, cols), f1)
    z = z.reshape(bsz, n1, 2 * n2, w)
    kc = 16
    y = pl.pallas_call(
        functools.partial(_dft_stage2_kernel, n2=n2),
        grid=(bsz, n1 // kc),
        in_specs=[pl.BlockSpec((1, kc, 2 * n2, w), lambda i, j: (i, j, 0, 0)),
                  pl.BlockSpec((kc, 2 * n2, 2 * n2), lambda i, j: (j, 0, 0)),
                  pl.BlockSpec(cs.shape, lambda i, j: (0, 0))],
        out_specs=pl.BlockSpec((1, kc, n2, w), lambda i, j: (i, j, 0, 0)),
        out_shape=jax.ShapeDtypeStruct((bsz, n1, n2, w), BF16),
        compiler_params=_cparams("arbitrary", "arbitrary"),
        name="fourier_stage2",
    )(z, m, cs)
    return y.transpose(0, 2, 1, 3).reshape(bsz, t, w)


def _scan_consts(rev):
    r = np.arange(CHUNK)
    ltri = ((r[None, :] >= r[:, None]) if rev else (r[None, :] <= r[:, None])).astype(np.float32)
    e = np.zeros((SUB * HGRN_EXPAND, HEAD_PAD), np.float32)
    for s in range(SUB):
        e[s * HGRN_EXPAND:(s + 1) * HGRN_EXPAND, s:CHUNK:SUB] = 1.0
    return jnp.asarray(ltri), jnp.asarray(e, BF16)


def _scan_kernel(hq_ref, hf_ref, hi_ref, lb_ref, s0_ref, ltri_ref, e_ref, o_ref, sfin_ref, s_scr, *, n_chunks, rev):
    step = pl.program_id(1)

    @pl.when(step == 0)
    def _():
        s_scr[...] = s0_ref[0]

    lb = lb_ref[...]
    ltri = ltri_ref[...]
    row = lax.broadcasted_iota(jnp.int32, (CHUNK, CHUNK), 0)
    col = lax.broadcasted_iota(jnp.int32, (CHUNK, CHUNK), 1)
    sub_shift = SUB.bit_length() - 1
    same_sub = (col >> sub_shift) == (row >> sub_shift)
    if rev:
        diag_mask = same_sub & ((col & (SUB - 1)) >= (row & (SUB - 1)))
    else:
        diag_mask = same_sub & ((col & (SUB - 1)) <= (row & (SUB - 1)))
    sub_row = lax.broadcasted_iota(jnp.int32, (N_SUB, SUB, HGRN_EXPAND), 1)
    col_low = lax.broadcasted_iota(jnp.int32, (SUB, CHUNK), 1)
    end_row = 0 if rev else CHUNK - 1
    nt = (((1,), (1,)), ((), ()))
    tn = (((0,), (0,)), ((), ()))

    def chunk(c, carry):
        if rev:
            c = n_chunks - 1 - c
        r0 = pl.multiple_of(c * CHUNK, CHUNK)
        f = lb + (1.0 - lb) * (1.0 / (1.0 + jnp.exp(-hf_ref[0, pl.ds(r0, CHUNK), :])))
        logf = jnp.log(f)
        kk = 1.0 - f
        g = jnp.dot(ltri, logf, precision=HIGHEST, preferred_element_type=F32)
        q = _silu(hq_ref[0, pl.ds(r0, CHUNK), :])
        v = hi_ref[0, pl.ds(r0, CHUNK), :]
        outs = []
        for hd in range(HGRN_HEADS):
            ks = slice(hd * HGRN_EXPAND, (hd + 1) * HGRN_EXPAND)
            vs = slice(hd * HGRN_HEAD_V, (hd + 1) * HGRN_HEAD_V)
            gh, qh, kh, vh = g[:, ks], q[:, ks], kk[:, ks], v[:, vs]
            vb = vh.astype(BF16)
            st_prev = s_scr[hd]
            g_end = gh[end_row:end_row + 1, :]
            o = lax.dot_general((qh * jnp.exp(gh)).astype(BF16), st_prev.astype(BF16), nt,
                                preferred_element_type=F32)
            k_dec = (kh * jnp.exp(g_end - gh)).astype(BF16)
            upd = lax.dot_general(vb, k_dec, tn, preferred_element_type=F32)
            s_scr[hd] = jnp.exp(g_end) * st_prev + upd
            g3 = gh.reshape(N_SUB, SUB, HGRN_EXPAND)
            h3 = (gh - jnp.log(jnp.maximum(kh, 0.0))).reshape(N_SUB, SUB, HGRN_EXPAND)
            q3 = qh.reshape(N_SUB, SUB, HGRN_EXPAND)
            slots = []
            for j in range(SUB):
                hj = jnp.broadcast_to(h3[:, j:j + 1, :], h3.shape)
                seen = (sub_row <= j) if rev else (sub_row >= j)
                dec = jnp.exp(jnp.where(seen, g3 - hj, -jnp.inf))
                slots.append((q3 * dec).reshape(CHUNK, HGRN_EXPAND).astype(BF16))
            pmat = jnp.concatenate(slots, axis=1)
            a = jnp.dot(pmat, e_ref[...], preferred_element_type=F32)[:, :CHUNK]
            a = jnp.where(diag_mask, a, 0.0)
            parts = []
            for i in range(N_SUB):
                rows = slice(i * SUB, (i + 1) * SUB)
                if (i == N_SUB - 1) if rev else (i == 0):
                    parts.append(a[rows])
                    continue
                ref_row = (i + 1) * SUB if rev else i * SUB - 1
                ref_g = gh[ref_row:ref_row + 1, :]
                qi = (qh[rows] * jnp.exp(gh[rows] - ref_g)).astype(BF16)
                kp = (kh * jnp.exp(jnp.minimum(ref_g - gh, 0.0))).astype(BF16)
                low = lax.dot_general(qi, kp, nt, preferred_element_type=F32)
                earlier = (col_low >= (i + 1) * SUB) if rev else (col_low < i * SUB)
                parts.append(a[rows] + jnp.where(earlier, low, 0.0))
            amat = jnp.concatenate(parts, axis=0)
            o = o + jnp.dot(amat.astype(BF16), vb, preferred_element_type=F32)
            outs.append(o)
        o_ref[0, pl.ds(r0, CHUNK), :] = jnp.concatenate(outs, axis=1)
        return carry

    lax.fori_loop(0, n_chunks, chunk, 0, unroll=2)

    @pl.when(step == pl.num_programs(1) - 1)
    def _():
        sfin_ref[0] = s_scr[...]


def hgrn_scan(hq, hf, hi, lb, s0, rev, tl=512):
    b, t, kw = hq.shape
    tl = min(tl, t)
    n_tiles = t // tl
    ltri, e = _scan_consts(rev)
    if rev:
        tok = lambda w: pl.BlockSpec((1, tl, w), lambda i, j: (i, n_tiles - 1 - j, 0))
    else:
        tok = lambda w: pl.BlockSpec((1, tl, w), lambda i, j: (i, j, 0))
    sspec = pl.BlockSpec((1, HGRN_HEADS, HGRN_HEAD_V, HGRN_EXPAND), lambda i, j: (i, 0, 0, 0))
    return pl.pallas_call(
        functools.partial(_scan_kernel, n_chunks=tl // CHUNK, rev=rev),
        grid=(b, n_tiles),
        in_specs=[tok(kw), tok(kw), tok(HGRN_WIDTH),
                  pl.BlockSpec(lb.shape, lambda i, j: (0, 0)), sspec,
                  pl.BlockSpec(ltri.shape, lambda i, j: (0, 0)),
                  pl.BlockSpec(e.shape, lambda i, j: (0, 0))],
        out_specs=[tok(HGRN_WIDTH), sspec],
        out_shape=[jax.ShapeDtypeStruct((b, t, HGRN_WIDTH), F32),
                   jax.ShapeDtypeStruct((b, HGRN_HEADS, HGRN_HEAD_V, HGRN_EXPAND), F32)],
        scratch_shapes=[pltpu.VMEM((HGRN_HEADS, HGRN_HEAD_V, HGRN_EXPAND), F32)],
        compiler_params=_cparams("arbitrary", "arbitrary"),
        name="hgrn_scan",
    )(hq, hf, hi, lb, s0, ltri, e)


def prep_out_weights(w_out):
    d = w_out.shape[1]
    wa = w_out[:MLA_HEADS * MLA_V].reshape(MLA_HEADS, MLA_V, d)
    wa = jnp.concatenate([wa, jnp.zeros((MLA_HEADS, HEAD_PAD - MLA_V, d), w_out.dtype)], axis=1)
    wa = wa.reshape(MLA_HEADS * HEAD_PAD, d)
    rest = w_out[MLA_HEADS * MLA_V:]
    return jnp.concatenate([wa, rest], axis=0).astype(BF16)


def _out_kernel(x_ref, att_ref, four_ref, of_ref, ob_ref, hg_ref, mod_ref, w_ref, gm_ref, gain_ref, x_out, h_out):
    hw = MLA_HEADS * HEAD_PAD
    o = of_ref[0] + ob_ref[0]
    ms = jnp.dot((o * o).astype(BF16), gm_ref[...], preferred_element_type=F32)
    rec = o * lax.rsqrt(ms + EPS) * gain_ref[...] * _silu(hg_ref[0])
    y = jnp.dot(att_ref[0], w_ref[:hw], preferred_element_type=F32)
    y = y + jnp.dot(four_ref[0], w_ref[hw:hw + FOURIER_WIDTH], preferred_element_type=F32)
    y = y + jnp.dot(rec.astype(BF16), w_ref[hw + FOURIER_WIDTH:], preferred_element_type=F32)
    x1 = x_ref[0] + mod_ref[0, 2:3, :] * y
    x_out[0] = x1
    h_out[0] = (_rms(x1) * (1.0 + mod_ref[0, 4:5, :]) + mod_ref[0, 3:4, :]).astype(h_out.dtype)


def out_proj(x, att, four, o_f, o_b, hg, mods, group_of_batch, w_out_ext, gain, tm):
    b, t, d = x.shape
    hw = MLA_HEADS * HEAD_PAD
    gm = jnp.asarray(np.kron(np.eye(HGRN_HEADS), np.full((HGRN_HEAD_V, HGRN_HEAD_V), 1.0 / HGRN_HEAD_V)), BF16)
    gain_t = jnp.tile(gain.astype(F32), HGRN_HEADS).reshape(1, HGRN_WIDTH)
    tok = lambda w: pl.BlockSpec((1, tm, w), lambda i, j: (i, j, 0))
    full2 = lambda a: pl.BlockSpec(a.shape, lambda i, j: (0, 0))
    if group_of_batch:
        mod_spec = pl.BlockSpec((1, N_MOD, d), lambda i, j: (i, 0, 0))
    else:
        mod_spec = pl.BlockSpec((1, N_MOD, d), lambda i, j: (0, 0, 0))
    return pl.pallas_call(
        _out_kernel,
        grid=(b, t // tm),
        in_specs=[tok(d), tok(hw), tok(FOURIER_WIDTH), tok(HGRN_WIDTH), tok(HGRN_WIDTH), tok(HGRN_WIDTH), mod_spec,
                  full2(w_out_ext), full2(gm), full2(gain_t)],
        out_specs=[tok(d), tok(d)],
        out_shape=[jax.ShapeDtypeStruct((b, t, d), F32), jax.ShapeDtypeStruct((b, t, d), BF16)],
        compiler_params=_cparams("arbitrary", "arbitrary"),
        name="out_proj",
    )(x, att, four, o_f, o_b, hg, mods, w_out_ext, gm, gain_t)


def _ffn_kernel(h_ref, x_ref, mod_ref, wg_ref, wu_ref, wd_ref, o_ref):
    j = pl.program_id(2)
    h = h_ref[0]
    a = jnp.dot(h, wg_ref[...], preferred_element_type=F32)
    u = jnp.dot(h, wu_ref[...], preferred_element_type=F32)
    y = jnp.dot((_silu(a) * u).astype(BF16), wd_ref[...], preferred_element_type=F32)
    gate = mod_ref[0, 5:6, :]

    @pl.when(j == 0)
    def _():
        o_ref[0] = x_ref[0] + gate * y

    @pl.when(j > 0)
    def _():
        o_ref[0] = o_ref[0] + gate * y


def ffn_dense(h, x, mods, group_of_batch, wg, wu, wd, tm, tf=1408):
    b, t, d = x.shape
    f = wg.shape[1]
    tok = lambda w: pl.BlockSpec((1, tm, w), lambda i, s, j: (i, s, 0))
    if group_of_batch:
        mod_spec = pl.BlockSpec((1, N_MOD, d), lambda i, s, j: (i, 0, 0))
    else:
        mod_spec = pl.BlockSpec((1, N_MOD, d), lambda i, s, j: (0, 0, 0))
    return pl.pallas_call(
        _ffn_kernel,
        grid=(b, t // tm, f // tf),
        in_specs=[tok(d), tok(d), mod_spec,
                  pl.BlockSpec((d, tf), lambda i, s, j: (0, j)),
                  pl.BlockSpec((d, tf), lambda i, s, j: (0, j)),
                  pl.BlockSpec((tf, d), lambda i, s, j: (j, 0))],
        out_specs=tok(d),
        out_shape=jax.ShapeDtypeStruct((b, t, d), F32),
        compiler_params=_cparams("arbitrary", "arbitrary", "arbitrary"),
        name="ffn_dense",
    )(h, x, mods, wg, wu, wd)


def _router_kernel(h_ref, wr_ref, g_ref):
    nt = (((1,), (1,)), ((), ()))
    logits = lax.dot_general(wr_ref[...], h_ref[0].astype(F32), nt, precision=HIGHEST, preferred_element_type=F32)
    e_id = lax.broadcasted_iota(jnp.int32, logits.shape, 0)
    m1 = jnp.max(logits, axis=0, keepdims=True)
    i1 = jnp.min(jnp.where(logits == m1, e_id, N_EXPERTS), axis=0, keepdims=True)
    rest = jnp.where(e_id == i1, -jnp.inf, logits)
    m2 = jnp.max(rest, axis=0, keepdims=True)
    i2 = jnp.min(jnp.where(rest == m2, e_id, N_EXPERTS), axis=0, keepdims=True)
    w2 = 1.0 / (1.0 + jnp.exp(m1 - m2))
    w1 = 1.0 - w2
    g_ref[0] = jnp.where(e_id == i1, w1, 0.0) + jnp.where(e_id == i2, w2, 0.0)


def router_gates(h, w_router_t, tm):
    b, t, d = h.shape
    return pl.pallas_call(
        _router_kernel,
        grid=(b, t // tm),
        in_specs=[pl.BlockSpec((1, tm, d), lambda i, j: (i, j, 0)),
                  pl.BlockSpec(w_router_t.shape, lambda i, j: (0, 0))],
        out_specs=pl.BlockSpec((1, N_EXPERTS, tm), lambda i, j: (i, 0, j)),
        out_shape=jax.ShapeDtypeStruct((b, N_EXPERTS, t), F32),
        compiler_params=_cparams("arbitrary", "arbitrary"),
        name="router",
    )(h, w_router_t)


MOE_CB = 128


def _moe_kernel(cnt_ref, h_ref, x_ref, gt_ref, g_ref, mod_ref, u_ref, wg_ref, wu_ref, wd_ref, o_ref,
                rank_scr, hc_scr, yc_scr, *, n_tiles):
    b, s, e, j = pl.program_id(0), pl.program_id(1), pl.program_id(2), pl.program_id(3)
    t = h_ref.shape[1]
    n_routed = cnt_ref[(b * n_tiles + s) * N_EXPERTS + e]
    n_blk = (n_routed + (MOE_CB - 1)) >> (MOE_CB.bit_length() - 1)
    sub = lax.broadcasted_iota(jnp.int32, (N_EXPERTS, t), 0)
    tn = (((0,), (0,)), ((), ()))

    @pl.when((e == 0) & (j == 0))
    def _():
        o_ref[0] = x_ref[0]
        sel = jnp.where(gt_ref[0] > 0.0, 1.0, 0.0).astype(BF16)
        rank_scr[...] = jnp.dot(sel, u_ref[...], preferred_element_type=F32)

    def one_hot(blk):
        gate = jnp.sum(jnp.where(sub == e, gt_ref[0], 0.0), axis=0, keepdims=True)
        rank = jnp.sum(jnp.where(sub == e, rank_scr[...], 0.0), axis=0, keepdims=True)
        rank = jnp.where(gate > 0.0, rank, -1.0)
        r = (lax.broadcasted_iota(jnp.int32, (MOE_CB, t), 0) + blk * MOE_CB).astype(F32)
        return jnp.where(r == rank, 1.0, 0.0).astype(BF16)

    @pl.when(j == 0)
    def _():
        def gather(blk, carry):
            r0 = pl.multiple_of(blk * MOE_CB, MOE_CB)
            hc = jnp.dot(one_hot(blk), h_ref[0], preferred_element_type=F32)
            hc_scr[pl.ds(r0, MOE_CB), :] = hc.astype(BF16)
            return carry
        lax.fori_loop(0, n_blk, gather, 0)

    def expert(blk, carry):
        r0 = pl.multiple_of(blk * MOE_CB, MOE_CB)
        hc = hc_scr[pl.ds(r0, MOE_CB), :]
        a = jnp.dot(hc, wg_ref[0], preferred_element_type=F32)
        u = jnp.dot(hc, wu_ref[0], preferred_element_type=F32)
        y = jnp.dot((_silu(a) * u).astype(BF16), wd_ref[0], preferred_element_type=F32)

        @pl.when(j == 0)
        def _():
            yc_scr[pl.ds(r0, MOE_CB), :] = y

        @pl.when(j > 0)
        def _():
            yc_scr[pl.ds(r0, MOE_CB), :] = yc_scr[pl.ds(r0, MOE_CB), :] + y
        return carry

    lax.fori_loop(0, n_blk, expert, 0)

    @pl.when(j == pl.num_programs(3) - 1)
    def _():
        lane = lax.broadcasted_iota(jnp.int32, g_ref.shape[1:], 1)
        ge = jnp.sum(jnp.where(lane == e, g_ref[0], 0.0), axis=1, keepdims=True)
        coef = mod_ref[0, 5:6, :] * ge

        def scatter(blk, carry):
            r0 = pl.multiple_of(blk * MOE_CB, MOE_CB)
            yc = yc_scr[pl.ds(r0, MOE_CB), :].astype(BF16)
            back = lax.dot_general(one_hot(blk), yc, tn, preferred_element_type=F32)
            o_ref[0] = o_ref[0] + coef * back
            return carry
        lax.fori_loop(0, n_blk, scatter, 0)


def moe_top2(h, x, gates, mods, group_of_batch, wg, wu, wd, tm, tf=1408):
    b, t, d = x.shape
    f = wg.shape[2]
    n_tiles = t // tm
    counts = jnp.sum((gates > 0.0).reshape(b, N_EXPERTS, n_tiles, tm), axis=-1, dtype=jnp.int32)
    counts = counts.transpose(0, 2, 1).reshape(-1)
    gates_lane = jnp.pad(gates.transpose(0, 2, 1), ((0, 0), (0, 0), (0, HEAD_PAD - N_EXPERTS)))
    r = np.arange(tm)
    upper = jnp.asarray(r[:, None] < r[None, :], BF16)
    tok = lambda w: pl.BlockSpec((1, tm, w), lambda i, s, e, j, c: (i, s, 0))
    if group_of_batch:
        mod_spec = pl.BlockSpec((1, N_MOD, d), lambda i, s, e, j, c: (i, 0, 0))
    else:
        mod_spec = pl.BlockSpec((1, N_MOD, d), lambda i, s, e, j, c: (0, 0, 0))
    grid_spec = pltpu.PrefetchScalarGridSpec(
        num_scalar_prefetch=1,
        grid=(b, n_tiles, N_EXPERTS, f // tf),
        in_specs=[tok(d), tok(d),
                  pl.BlockSpec((1, N_EXPERTS, tm), lambda i, s, e, j, c: (i, 0, s)),
                  tok(HEAD_PAD), mod_spec,
                  pl.BlockSpec((tm, tm), lambda i, s, e, j, c: (0, 0)),
                  pl.BlockSpec((1, d, tf), lambda i, s, e, j, c: (e, 0, j)),
                  pl.BlockSpec((1, d, tf), lambda i, s, e, j, c: (e, 0, j)),
                  pl.BlockSpec((1, tf, d), lambda i, s, e, j, c: (e, j, 0))],
        out_specs=tok(d),
        scratch_shapes=[pltpu.VMEM((N_EXPERTS, tm), F32), pltpu.VMEM((tm, d), BF16), pltpu.VMEM((tm, d), F32)],
    )
    return pl.pallas_call(
        functools.partial(_moe_kernel, n_tiles=n_tiles),
        grid_spec=grid_spec,
        out_shape=jax.ShapeDtypeStruct((b, t, d), F32),
        compiler_params=_cparams("arbitrary", "arbitrary", "arbitrary", "arbitrary"),
        name="moe_top2",
    )(counts, h, x, gates, gates_lane, mods, upper, wg, wu, wd)


def _final_kernel(x_ref, g_ref, o_ref):
    o_ref[0] = _rms(x_ref[0]) * g_ref[...]


def final_norm(x, gain, tm=1024):
    b, t, d = x.shape
    return pl.pallas_call(
        _final_kernel,
        grid=(b, t // tm),
        in_specs=[pl.BlockSpec((1, tm, d), lambda i, j: (i, j, 0)), pl.BlockSpec((1, d), lambda i, j: (0, 0))],
        out_specs=pl.BlockSpec((1, tm, d), lambda i, j: (i, j, 0)),
        out_shape=jax.ShapeDtypeStruct((b, t, d), F32),
        compiler_params=_cparams("arbitrary", "arbitrary"),
        name="final_norm",
    )(x, gain.reshape(1, d))


def _lower_bounds(lb_logits):
    p = jax.nn.softmax(lb_logits.astype(F32), axis=1)
    return jnp.cumsum(p, axis=1) - p[:, :1]


def _bidirectional_scan(hq, hf, hb, hi, lb_f, lb_b, s0_f, s0_b):
    o_f, s_f = hgrn_scan(hq, hf, hi, lb_f, s0_f, False)
    o_b, s_b = hgrn_scan(hq, hb, hi, lb_b, s0_b, True)
    return o_f, o_b, s_f, s_b


def _channel_mix(layer, h, x, mods, by_batch, ffn_w, moe_w, tm):
    j = layer // 2
    if layer % 2 == 0:
        wg, wu, wd = ffn_w
        return ffn_dense(h, x, mods, by_batch, wg[j].astype(BF16), wu[j].astype(BF16), wd[j].astype(BF16), tm)
    router, wg, wu, wd = moe_w
    gates = router_gates(h, router[j].T.astype(F32), tm)
    return moe_top2(h, x, gates, mods, by_batch, wg[j].astype(BF16), wu[j].astype(BF16), wd[j].astype(BF16),
                    min(x.shape[1], 1024))


def kernel(x, c, ctx, c_ctx, w_ada, b_ada, w_in, q_norm_g, kv_norm_g, w_uq, w_ukv, hgrn_norm_g, lb_logits, w_out,
           ffn_w_gate, ffn_w_up, ffn_w_down, moe_router, moe_w_gate, moe_w_up, moe_w_down, final_g):
    bsz, n_lat, d = x.shape
    n_ctx = ctx.shape[1]
    depth = w_ada.shape[0]
    x = x.astype(F32)
    ctx = ctx.astype(F32)

    n_groups = 8
    cond = jnp.zeros((n_groups, d), F32).at[:bsz].set(c.astype(F32)).at[bsz].set(c_ctx.astype(F32))
    mods_all = ada_mods(cond, w_ada.astype(F32), b_ada.astype(F32)).reshape(depth, n_groups, N_MOD, d)
    lbs = _lower_bounds(lb_logits)
    tabs_lat = rope_tables(n_lat)
    tabs_ctx = identity_rope_tables(n_ctx)
    zero_state = jnp.zeros((bsz, HGRN_HEADS, HGRN_HEAD_V, HGRN_EXPAND), F32)
    ffn_w = (ffn_w_gate, ffn_w_up, ffn_w_down)
    moe_w = (moe_router, moe_w_gate, moe_w_up, moe_w_down)
    tm_lat = 512
    tm_ctx = n_ctx

    for layer in range(depth):
        need_ctx = layer < depth - 1
        mods_lat = mods_all[layer, :bsz]
        mods_ctx = mods_all[layer, bsz:bsz + 1]
        w_ext, wq_ext, wkv_ext = prep_in_weights(w_in[layer], w_uq[layer], w_ukv[layer])
        gq = q_norm_g[layer].astype(F32).reshape(1, -1)
        gkv = kv_norm_g[layer].astype(F32).reshape(1, -1)
        lb_f = lbs[0, layer].reshape(1, -1)
        lb_b = lbs[1, layer].reshape(1, -1)
        w_out_ext = prep_out_weights(w_out[layer])

        q_c, k_c, v_c, fo_c, hq_c, hf_c, hb_c, hi_c, hg_c = in_proj(
            ctx, mods_ctx, False, w_ext, gq, gkv, wq_ext, wkv_ext, tabs_ctx, tm_ctx)
        q_l, k_l, v_l, fo_l, hq_l, hf_l, hb_l, hi_l, hg_l = in_proj(
            x, mods_lat, True, w_ext, gq, gkv, wq_ext, wkv_ext, tabs_lat, tm_lat)

        att_l = attention(q_l, k_c, v_c, k_l, v_l)
        four_l = fourier_large(fo_l)
        oc_f, oc_b, sc_f, sc_b = _bidirectional_scan(hq_c, hf_c, hb_c, hi_c, lb_f, lb_b, zero_state, zero_state)
        ol_f, ol_b, _, _ = _bidirectional_scan(hq_l, hf_l, hb_l, hi_l, lb_f, lb_b, sc_f, sc_b)
        x, h_l = out_proj(x, att_l, four_l, ol_f, ol_b, hg_l, mods_lat, True, w_out_ext, hgrn_norm_g[layer], tm_lat)
        if need_ctx:
            att_c = attention(q_c, k_c, v_c)
            four_c = fourier_small(fo_c)
            ctx, h_c = out_proj(ctx, att_c, four_c, oc_f, oc_b, hg_c, mods_ctx, False, w_out_ext,
                                hgrn_norm_g[layer], tm_ctx)
            ctx = _channel_mix(layer, h_c, ctx, mods_ctx, False, ffn_w, moe_w, tm_ctx)
        x = _channel_mix(layer, h_l, x, mods_lat, True, ffn_w, moe_w, tm_lat)

    return final_norm(x, final_g.astype(F32))
```

```python
import functools

import numpy as np
import jax
import jax.numpy as jnp
from jax import lax
from jax.experimental import pallas as pl
from jax.experimental.pallas import tpu as pltpu

F32 = jnp.float32
BF16 = jnp.bfloat16
HIGHEST = lax.Precision.HIGHEST

EPS = 1e-6
N_MOD = 6
GRID_W = 64
ROPE_BASE = 10000.0

MLA_HEADS = 8
MLA_Q_RANK = 256
MLA_KV_RANK = 128
MLA_NOPE = 64
MLA_ROPE = 32
MLA_V = 64
MLA_SCALE = (MLA_NOPE + MLA_ROPE) ** -0.5
Q_SCALE = MLA_SCALE * float(np.log2(np.e))
HEAD_PAD = 128
PV_ROWS = 80

FOURIER_GROUPS = 4
FOURIER_CH = 64
FOURIER_WIDTH = FOURIER_GROUPS * FOURIER_CH

HGRN_HEADS = 4
HGRN_EXPAND = 128
HGRN_HEAD_V = 64
HGRN_KEY_WIDTH = HGRN_HEADS * HGRN_EXPAND
HGRN_WIDTH = HGRN_HEADS * HGRN_HEAD_V
CHUNK = 64
SUB = 16
N_SUB = CHUNK // SUB

N_EXPERTS = 8
TOP_K = 2

IN_SIZES = (MLA_Q_RANK, MLA_KV_RANK, MLA_ROPE, FOURIER_WIDTH, HGRN_KEY_WIDTH, HGRN_KEY_WIDTH, HGRN_KEY_WIDTH,
            HGRN_WIDTH, HGRN_WIDTH)

V7X_VMEM_LIMIT = 56 * 1024 * 1024


def _cparams(*sem):
    return pltpu.CompilerParams(dimension_semantics=sem, vmem_limit_bytes=V7X_VMEM_LIMIT)


def _silu(x):
    return x * (1.0 / (1.0 + jnp.exp(-x)))


def _rms(x):
    return x * lax.rsqrt(jnp.mean(x * x, axis=-1, keepdims=True) + EPS)


def _ada_kernel(c_ref, w_ref, b_ref, o_ref):
    c = _silu(c_ref[...])
    o_ref[0] = jnp.dot(c, w_ref[0], precision=HIGHEST, preferred_element_type=F32) + b_ref[0]


def ada_mods(cond, w_ada, b_ada):
    depth, d, n = w_ada.shape
    g = cond.shape[0]
    tn = 1536
    return pl.pallas_call(
        _ada_kernel,
        grid=(depth, n // tn),
        in_specs=[pl.BlockSpec((g, d), lambda l, j: (0, 0)),
                  pl.BlockSpec((1, d, tn), lambda l, j: (l, 0, j)),
                  pl.BlockSpec((1, 1, tn), lambda l, j: (l, 0, j))],
        out_specs=pl.BlockSpec((1, g, tn), lambda l, j: (l, 0, j)),
        out_shape=jax.ShapeDtypeStruct((depth, g, n), F32),
        compiler_params=_cparams("arbitrary", "arbitrary"),
        name="ada_mods",
    )(cond, w_ada, b_ada.reshape(depth, 1, n))


C_DQ = 0
C_DKV = C_DQ + MLA_Q_RANK
C_KRA = C_DKV + MLA_KV_RANK
C_KRB = C_KRA + HEAD_PAD
C_FO = C_KRB + HEAD_PAD
C_HQ = C_FO + FOURIER_WIDTH
C_HF = C_HQ + HGRN_KEY_WIDTH
C_HB = C_HF + HGRN_KEY_WIDTH
C_HI = C_HB + HGRN_KEY_WIDTH
C_HG = C_HI + HGRN_WIDTH
C_END = C_HG + HGRN_WIDTH


def _rope_partner():
    i = np.arange(MLA_ROPE)
    half = (i % 16) // 8
    return np.where(half == 0, i + 8, i - 8)


def prep_in_weights(w_in, w_uq, w_ukv):
    d = w_in.shape[0]
    starts = np.cumsum((0,) + IN_SIZES)
    dq, dkv, kr, fo, hq, hf, hb, hi, hg = [w_in[:, starts[i]:starts[i + 1]] for i in range(len(IN_SIZES))]
    z64 = jnp.zeros((d, MLA_NOPE), w_in.dtype)
    z32 = jnp.zeros((d, HEAD_PAD - MLA_NOPE - MLA_ROPE), w_in.dtype)
    kra = jnp.concatenate([z64, kr, z32], axis=1)
    krb = jnp.concatenate([z64, kr[:, _rope_partner()], z32], axis=1)
    w_ext = jnp.concatenate([dq, dkv, kra, krb, fo, hq, hf, hb, hi, hg], axis=1).astype(BF16)

    r = w_uq.shape[0]
    wq = w_uq.reshape(r, MLA_HEADS, MLA_NOPE + MLA_ROPE)
    zq = jnp.zeros((r, MLA_HEADS, HEAD_PAD - MLA_NOPE - MLA_ROPE), w_uq.dtype)
    wqa = jnp.concatenate([wq, zq], axis=2).reshape(r, MLA_HEADS * HEAD_PAD)
    wq_sw = jnp.concatenate([jnp.zeros((r, MLA_HEADS, MLA_NOPE), w_uq.dtype),
                             wq[:, :, MLA_NOPE:][:, :, _rope_partner()], zq], axis=2)
    wqb = wq_sw.reshape(r, MLA_HEADS * HEAD_PAD)
    wq_ext = jnp.concatenate([wqa, wqb], axis=1).astype(BF16)

    rk = w_ukv.shape[0]
    wkv = w_ukv.reshape(rk, MLA_HEADS, MLA_NOPE + MLA_V)
    zk = jnp.zeros((rk, MLA_HEADS, HEAD_PAD - MLA_NOPE), w_ukv.dtype)
    wk = jnp.concatenate([wkv[:, :, :MLA_NOPE], zk], axis=2).reshape(rk, MLA_HEADS * HEAD_PAD)
    zv = jnp.zeros((rk, MLA_HEADS, HEAD_PAD - MLA_V), w_ukv.dtype)
    wv = jnp.concatenate([wkv[:, :, MLA_NOPE:], zv], axis=2).reshape(rk, MLA_HEADS * HEAD_PAD)
    wkv_ext = jnp.concatenate([wk, wv], axis=1).astype(BF16)
    return w_ext, wq_ext, wkv_ext


def rope_tables(n_tokens):
    rows = n_tokens // GRID_W
    row_ids = jnp.repeat(jnp.arange(rows, dtype=F32), GRID_W)
    col_ids = jnp.tile(jnp.arange(GRID_W, dtype=F32), rows)
    per_axis = MLA_ROPE // 2
    inv_freq = ROPE_BASE ** (-jnp.arange(0, per_axis, 2, dtype=F32) / per_axis)
    ang = jnp.concatenate([row_ids[:, None] * inv_freq, col_ids[:, None] * inv_freq], axis=-1)
    cos, sin = jnp.cos(ang), jnp.sin(ang)
    i = np.arange(MLA_ROPE)
    tab_idx = (i // 16) * 8 + (i % 8)
    sign = np.where((i % 16) // 8 == 0, -1.0, 1.0).astype(np.float32)
    c32 = cos[:, tab_idx]
    s32 = sin[:, tab_idx] * sign
    ones = jnp.ones((n_tokens, MLA_NOPE), F32)
    zeros = jnp.zeros((n_tokens, MLA_NOPE), F32)
    pad = jnp.zeros((n_tokens, HEAD_PAD - MLA_NOPE - MLA_ROPE), F32)
    ck = jnp.concatenate([ones, c32, pad], axis=1)
    sk = jnp.concatenate([zeros, s32, pad], axis=1)
    return ck * Q_SCALE, sk * Q_SCALE, ck, sk


def identity_rope_tables(n_tokens):
    lane = np.arange(HEAD_PAD)
    ck = jnp.asarray(np.broadcast_to((lane < MLA_NOPE + MLA_ROPE).astype(np.float32), (n_tokens, HEAD_PAD)))
    sk = jnp.zeros((n_tokens, HEAD_PAD), F32)
    return ck * Q_SCALE, sk, ck, sk


def _in_kernel(x_ref, mod_ref, w_ref, gq_ref, gkv_ref, wqt_ref, wk_ref, wvt_ref, cqt_ref, sqt_ref, ck_ref, sk_ref,
               qt_out, k_out, vt_out, fo_out, hq_out, hf_out, hb_out, hi_out, hg_out):
    x = x_ref[0]
    shift = mod_ref[0, 0:1, :]
    scale = mod_ref[0, 1:2, :]
    h = (_rms(x) * (1.0 + scale) + shift).astype(BF16)

    def proj(c0, c1):
        return jnp.dot(h, w_ref[:, c0:c1], preferred_element_type=F32)

    fo_out[0] = proj(C_FO, C_HQ).astype(fo_out.dtype)
    hq_out[0] = proj(C_HQ, C_HF)
    hf_out[0] = proj(C_HF, C_HB)
    hb_out[0] = proj(C_HB, C_HI)
    hi_out[0] = proj(C_HI, C_HG)
    hg_out[0] = proj(C_HG, C_END)

    hw = MLA_HEADS * HEAD_PAD
    nt = (((1,), (1,)), ((), ()))
    cq = (_rms(proj(C_DQ, C_DKV)) * gq_ref[...]).astype(BF16)
    qt2 = lax.dot_general(wqt_ref[...], cq, nt, preferred_element_type=F32)
    ckv = (_rms(proj(C_DKV, C_KRA)) * gkv_ref[...]).astype(BF16)
    kn = jnp.dot(ckv, wk_ref[...], preferred_element_type=F32)
    vt = lax.dot_general(wvt_ref[...], ckv, nt, preferred_element_type=F32)
    kr = proj(C_KRA, C_KRB) * ck_ref[...] + proj(C_KRB, C_FO) * sk_ref[...]
    cqt = cqt_ref[...]
    sqt = sqt_ref[...]
    for hd in range(MLA_HEADS):
        sl = slice(hd * HEAD_PAD, (hd + 1) * HEAD_PAD)
        sl2 = slice(hw + hd * HEAD_PAD, hw + (hd + 1) * HEAD_PAD)
        qt_out[0, sl, :] = (qt2[sl] * cqt + qt2[sl2] * sqt).astype(qt_out.dtype)
        k_out[0, :, sl] = (kn[:, sl] + kr).astype(k_out.dtype)
    row = lax.broadcasted_iota(jnp.int32, vt.shape, 0)
    vt_out[0] = jnp.where((row & (HEAD_PAD - 1)) == MLA_V, 1.0, vt).astype(vt_out.dtype)


def in_proj(x, mods, group_of_batch, w_ext, gq, gkv, wq_ext, wkv_ext, tabs, tm):
    b, t, d = x.shape
    hw = MLA_HEADS * HEAD_PAD
    cq, sq, ck, sk = tabs
    wqt = wq_ext.T
    wk = wkv_ext[:, :hw]
    wvt = wkv_ext[:, hw:].T
    grid = (b, t // tm)
    tok = lambda w: pl.BlockSpec((1, tm, w), lambda i, j: (i, j, 0))
    tok_t = pl.BlockSpec((1, hw, tm), lambda i, j: (i, 0, j))
    full2 = lambda a: pl.BlockSpec(a.shape, lambda i, j: (0, 0))
    tab = pl.BlockSpec((tm, HEAD_PAD), lambda i, j: (j, 0))
    tab_t = pl.BlockSpec((HEAD_PAD, tm), lambda i, j: (0, j))
    if group_of_batch:
        mod_spec = pl.BlockSpec((1, N_MOD, d), lambda i, j: (i, 0, 0))
    else:
        mod_spec = pl.BlockSpec((1, N_MOD, d), lambda i, j: (0, 0, 0))
    outs = [(FOURIER_WIDTH, F32), (HGRN_KEY_WIDTH, F32), (HGRN_KEY_WIDTH, F32),
            (HGRN_KEY_WIDTH, F32), (HGRN_WIDTH, F32), (HGRN_WIDTH, F32)]
    return pl.pallas_call(
        _in_kernel,
        grid=grid,
        in_specs=[tok(d), mod_spec, full2(w_ext), full2(gq), full2(gkv), full2(wqt), full2(wk), full2(wvt),
                  tab_t, tab_t, tab, tab],
        out_specs=[tok_t, tok(hw), tok_t] + [tok(w) for w, _ in outs],
        out_shape=[jax.ShapeDtypeStruct((b, hw, t), BF16), jax.ShapeDtypeStruct((b, t, hw), BF16),
                   jax.ShapeDtypeStruct((b, hw, t), BF16)]
                  + [jax.ShapeDtypeStruct((b, t, w), dt) for w, dt in outs],
        compiler_params=_cparams("arbitrary", "arbitrary"),
        name="in_proj",
    )(x, mods, w_ext, gq, gkv, wqt, wk, wvt, cq.T, sq.T, ck, sk)


def _attn_kernel(*refs, tk, n_lat):
    if n_lat:
        qt_ref, kc_ref, vtc_ref, kl_ref, vtl_ref, o_ref, m_scr, acc_scr, s_scr = refs
    else:
        qt_ref, kc_ref, vtc_ref, o_ref, m_scr, acc_scr = refs
    qt = qt_ref[0]

    def softmax_pv(st, vt):
        m_old = m_scr[...]
        m_new = jnp.maximum(m_old, jnp.max(st, axis=0, keepdims=True))
        pt = jnp.exp2(st - m_new).astype(BF16)
        acc_scr[...] = acc_scr[...] * jnp.exp2(m_old - m_new) + jnp.dot(vt, pt, preferred_element_type=F32)
        m_scr[...] = m_new

    m_scr[...] = jnp.full(m_scr.shape, -1e30, F32)
    acc_scr[...] = jnp.zeros(acc_scr.shape, F32)
    softmax_pv(jnp.dot(kc_ref[0], qt, preferred_element_type=F32), vtc_ref[0, :PV_ROWS, :])
    if n_lat:
        n_blk = n_lat // tk

        def scores(i):
            start = pl.multiple_of(i * tk, tk)
            return jnp.dot(kl_ref[0, pl.ds(start, tk), :], qt, preferred_element_type=F32)

        def values(i):
            return vtl_ref[0, :PV_ROWS, pl.ds(pl.multiple_of(i * tk, tk), tk)]

        s_scr[0] = scores(0)

        def body(p, carry):
            i = 2 * p
            s_scr[1] = scores(i + 1)
            softmax_pv(s_scr[0], values(i))
            s_scr[0] = scores(jnp.minimum(i + 2, n_blk - 1))
            softmax_pv(s_scr[1], values(i + 1))
            return carry
        lax.fori_loop(0, n_blk // 2, body, 0)
    acc = acc_scr[...]
    out_t = acc * (1.0 / acc[MLA_V:MLA_V + 1, :])
    out_t = jnp.concatenate([out_t, jnp.zeros((HEAD_PAD - PV_ROWS, out_t.shape[1]), F32)], axis=0)
    o_ref[0] = out_t.T.astype(o_ref.dtype)


def attention(qt, k_ctx, vt_ctx, k_lat=None, vt_lat=None, tq=1024, tk=1024):
    b, hw, t = qt.shape
    n_ctx = k_ctx.shape[1]
    n_lat = 0 if k_lat is None else k_lat.shape[1]
    tq = min(tq, t)
    scratch = [pltpu.VMEM((1, tq), F32), pltpu.VMEM((PV_ROWS, tq), F32)]
    if n_lat:
        tk = min(tk, n_lat // 2)
        assert n_lat % (2 * tk) == 0
        scratch.append(pltpu.VMEM((2, tk, tq), F32))
    grid = (b, MLA_HEADS, t // tq)
    qspec = pl.BlockSpec((1, HEAD_PAD, tq), lambda i, h, j: (i, h, j))
    in_specs = [qspec,
                pl.BlockSpec((1, n_ctx, HEAD_PAD), lambda i, h, j: (i, 0, h)),
                pl.BlockSpec((1, HEAD_PAD, n_ctx), lambda i, h, j: (i, h, 0))]
    args = [qt, k_ctx, vt_ctx]
    if n_lat:
        in_specs += [pl.BlockSpec((1, n_lat, HEAD_PAD), lambda i, h, j: (i, 0, h)),
                     pl.BlockSpec((1, HEAD_PAD, n_lat), lambda i, h, j: (i, h, 0))]
        args += [k_lat, vt_lat]
    return pl.pallas_call(
        functools.partial(_attn_kernel, tk=tk, n_lat=n_lat),
        grid=grid,
        in_specs=in_specs,
        out_specs=pl.BlockSpec((1, tq, HEAD_PAD), lambda i, h, j: (i, j, h)),
        out_shape=jax.ShapeDtypeStruct((b, t, hw), BF16),
        scratch_shapes=scratch,
        compiler_params=_cparams("arbitrary", "arbitrary", "arbitrary"),
        name="attention",
    )(*args)


def _channel_dft_mats(norm):
    c = np.arange(FOURIER_CH)
    ang = 2.0 * np.pi * np.outer(c, c) / FOURIER_CH
    eye = np.eye(FOURIER_GROUPS)
    cc = np.kron(eye, np.cos(ang)) * norm
    sc = np.kron(eye, np.sin(ang)) * norm
    return np.concatenate([cc, sc], axis=0).astype(np.float32)


def _dft_small_kernel(u_ref, f_ref, cs_ref, o_ref):
    t = u_ref.shape[1]
    x = jnp.dot(f_ref[...], u_ref[0].astype(BF16), preferred_element_type=F32)
    w = FOURIER_WIDTH
    y = jnp.dot(x[:t].astype(BF16), cs_ref[:w], preferred_element_type=F32)
    y = y + jnp.dot(x[t:].astype(BF16), cs_ref[w:], preferred_element_type=F32)
    o_ref[0] = y.astype(o_ref.dtype)


def fourier_small(u):
    b, t, w = u.shape
    n = np.arange(t)
    ang = 2.0 * np.pi * (np.outer(n, n) % t) / t
    f = jnp.asarray(np.concatenate([np.cos(ang), -np.sin(ang)], axis=0), BF16)
    cs = jnp.asarray(_channel_dft_mats((t * FOURIER_CH) ** -0.5), BF16)
    return pl.pallas_call(
        _dft_small_kernel,
        grid=(b,),
        in_specs=[pl.BlockSpec((1, t, w), lambda i: (i, 0, 0)),
                  pl.BlockSpec(f.shape, lambda i: (0, 0)),
                  pl.BlockSpec(cs.shape, lambda i: (0, 0))],
        out_specs=pl.BlockSpec((1, t, w), lambda i: (i, 0, 0)),
        out_shape=jax.ShapeDtypeStruct((b, t, w), BF16),
        compiler_params=_cparams("arbitrary"),
        name="fourier_ctx",
    )(u, f, cs)


def _dft_stage1_kernel(x_ref, f_ref, o_ref):
    o_ref[0] = jnp.dot(f_ref[...], x_ref[0].astype(BF16), preferred_element_type=F32).astype(o_ref.dtype)


def _dft_stage2_kernel(z_ref, m_ref, cs_ref, o_ref, *, n2):
    w = FOURIER_WIDTH
    for j in range(z_ref.shape[1]):
        x = jnp.dot(m_ref[j], z_ref[0, j], preferred_element_type=F32)
        y = jnp.dot(x[:n2].astype(BF16), cs_ref[:w], preferred_element_type=F32)
        y = y + jnp.dot(x[n2:].astype(BF16), cs_ref[w:], preferred_element_type=F32)
        o_ref[0, j] = y.astype(o_ref.dtype)


def fourier_large(u, n1=128):
    bsz, t, w = u.shape
    n2 = t // n1
    a = np.arange(n1)
    ang1 = 2.0 * np.pi * np.outer(a, a) / n1
    f1 = np.empty((2 * n1, n1), np.float64)
    f1[0::2] = np.cos(ang1)
    f1[1::2] = -np.sin(ang1)
    f1 = jnp.asarray(f1, BF16)
    k1 = np.arange(n1)[:, None, None]
    k2 = np.arange(n2)[None, :, None]
    bb = np.arange(n2)[None, None, :]
    ang2 = 2.0 * np.pi * ((bb * (k1 + n1 * k2)) % t) / t
    mr, mi = np.cos(ang2), np.sin(ang2)
    m = np.concatenate([np.concatenate([mr, mi], axis=2), np.concatenate([-mi, mr], axis=2)], axis=1)
    m = jnp.asarray(m, BF16)
    cs = jnp.asarray(_channel_dft_mats((t * FOURIER_CH) ** -0.5), BF16)

    cols = n2 * w
    tc = 2048
    z = pl.pallas_call(
        _dft_stage1_kernel,
        grid=(bsz, cols // tc),
        in_specs=[pl.BlockSpec((1, n1, tc), lambda i, j: (i, 0, j)),
                  pl.BlockSpec(f1.shape, lambda i, j: (0, 0))],
        out_specs=pl.BlockSpec((1, 2 * n1, tc), lambda i, j: (i, 0, j)),
        out_shape=jax.ShapeDtypeStruct((bsz, 2 * n1, cols), BF16),
        compiler_params=_cparams("arbitrary", "arbitrary"),
        name="fourier_stage1",
    )(u.reshape(bsz, n1, cols), f1)
    z = z.reshape(bsz, n1, 2 * n2, w)
    kc = 16
    y = pl.pallas_call(
        functools.partial(_dft_stage2_kernel, n2=n2),
        grid=(bsz, n1 // kc),
        in_specs=[pl.BlockSpec((1, kc, 2 * n2, w), lambda i, j: (i, j, 0, 0)),
                  pl.BlockSpec((kc, 2 * n2, 2 * n2), lambda i, j: (j, 0, 0)),
                  pl.BlockSpec(cs.shape, lambda i, j: (0, 0))],
        out_specs=pl.BlockSpec((1, kc, n2, w), lambda i, j: (i, j, 0, 0)),
        out_shape=jax.ShapeDtypeStruct((bsz, n1, n2, w), BF16),
        compiler_params=_cparams("arbitrary", "arbitrary"),
        name="fourier_stage2",
    )(z, m, cs)
    return y.transpose(0, 2, 1, 3).reshape(bsz, t, w)


def _scan_consts(rev):
    r = np.arange(CHUNK)
    ltri = ((r[None, :] >= r[:, None]) if rev else (r[None, :] <= r[:, None])).astype(np.float32)
    e = np.zeros((SUB * HGRN_EXPAND, HEAD_PAD), np.float32)
    for s in range(SUB):
        e[s * HGRN_EXPAND:(s + 1) * HGRN_EXPAND, s:CHUNK:SUB] = 1.0
    return jnp.asarray(ltri), jnp.asarray(e, BF16)


def _scan_kernel(hq_ref, hf_ref, hi_ref, lb_ref, s0_ref, ltri_ref, e_ref, o_ref, sfin_ref, s_scr, *, n_chunks, rev):
    step = pl.program_id(1)

    @pl.when(step == 0)
    def _():
        s_scr[...] = s0_ref[0]

    lb = lb_ref[...]
    ltri = ltri_ref[...]
    row = lax.broadcasted_iota(jnp.int32, (CHUNK, CHUNK), 0)
    col = lax.broadcasted_iota(jnp.int32, (CHUNK, CHUNK), 1)
    sub_shift = SUB.bit_length() - 1
    same_sub = (col >> sub_shift) == (row >> sub_shift)
    if rev:
        diag_mask = same_sub & ((col & (SUB - 1)) >= (row & (SUB - 1)))
    else:
        diag_mask = same_sub & ((col & (SUB - 1)) <= (row & (SUB - 1)))
    sub_row = lax.broadcasted_iota(jnp.int32, (N_SUB, SUB, HGRN_EXPAND), 1)
    col_low = lax.broadcasted_iota(jnp.int32, (SUB, CHUNK), 1)
    end_row = 0 if rev else CHUNK - 1
    nt = (((1,), (1,)), ((), ()))
    tn = (((0,), (0,)), ((), ()))

    def chunk(c, carry):
        if rev:
            c = n_chunks - 1 - c
        r0 = pl.multiple_of(c * CHUNK, CHUNK)
        f = lb + (1.0 - lb) * (1.0 / (1.0 + jnp.exp(-hf_ref[0, pl.ds(r0, CHUNK), :])))
        logf = jnp.log(f)
        kk = 1.0 - f
        g = jnp.dot(ltri, logf, precision=HIGHEST, preferred_element_type=F32)
        q = _silu(hq_ref[0, pl.ds(r0, CHUNK), :])
        v = hi_ref[0, pl.ds(r0, CHUNK), :]
        outs = []
        for hd in range(HGRN_HEADS):
            ks = slice(hd * HGRN_EXPAND, (hd + 1) * HGRN_EXPAND)
            vs = slice(hd * HGRN_HEAD_V, (hd + 1) * HGRN_HEAD_V)
            gh, qh, kh, vh = g[:, ks], q[:, ks], kk[:, ks], v[:, vs]
            vb = vh.astype(BF16)
            st_prev = s_scr[hd]
            g_end = gh[end_row:end_row + 1, :]
            o = lax.dot_general((qh * jnp.exp(gh)).astype(BF16), st_prev.astype(BF16), nt,
                                preferred_element_type=F32)
            k_dec = (kh * jnp.exp(g_end - gh)).astype(BF16)
            upd = lax.dot_general(vb, k_dec, tn, preferred_element_type=F32)
            s_scr[hd] = jnp.exp(g_end) * st_prev + upd
            g3 = gh.reshape(N_SUB, SUB, HGRN_EXPAND)
            h3 = (gh - jnp.log(jnp.maximum(kh, 0.0))).reshape(N_SUB, SUB, HGRN_EXPAND)
            q3 = qh.reshape(N_SUB, SUB, HGRN_EXPAND)
            slots = []
            for j in range(SUB):
                hj = jnp.broadcast_to(h3[:, j:j + 1, :], h3.shape)
                seen = (sub_row <= j) if rev else (sub_row >= j)
                dec = jnp.exp(jnp.where(seen, g3 - hj, -jnp.inf))
                slots.append((q3 * dec).reshape(CHUNK, HGRN_EXPAND).astype(BF16))
            pmat = jnp.concatenate(slots, axis=1)
            a = jnp.dot(pmat, e_ref[...], preferred_element_type=F32)[:, :CHUNK]
            a = jnp.where(diag_mask, a, 0.0)
            parts = []
            for i in range(N_SUB):
                rows = slice(i * SUB, (i + 1) * SUB)
                if (i == N_SUB - 1) if rev else (i == 0):
                    parts.append(a[rows])
                    continue
                ref_row = (i + 1) * SUB if rev else i * SUB - 1
                ref_g = gh[ref_row:ref_row + 1, :]
                qi = (qh[rows] * jnp.exp(gh[rows] - ref_g)).astype(BF16)
                kp = (kh * jnp.exp(jnp.minimum(ref_g - gh, 0.0))).astype(BF16)
                low = lax.dot_general(qi, kp, nt, preferred_element_type=F32)
                earlier = (col_low >= (i + 1) * SUB) if rev else (col_low < i * SUB)
                parts.append(a[rows] + jnp.where(earlier, low, 0.0))
            amat = jnp.concatenate(parts, axis=0)
            o = o + jnp.dot(amat.astype(BF16), vb, preferred_element_type=F32)
            outs.append(o)
        o_ref[0, pl.ds(r0, CHUNK), :] = jnp.concatenate(outs, axis=1)
        return carry

    lax.fori_loop(0, n_chunks, chunk, 0, unroll=4)

    @pl.when(step == pl.num_programs(1) - 1)
    def _():
        sfin_ref[0] = s_scr[...]


def hgrn_scan(hq, hf, hi, lb, s0, rev, tl=512):
    b, t, kw = hq.shape
    tl = min(tl, t)
    n_tiles = t // tl
    ltri, e = _scan_consts(rev)
    if rev:
        tok = lambda w: pl.BlockSpec((1, tl, w), lambda i, j: (i, n_tiles - 1 - j, 0))
    else:
        tok = lambda w: pl.BlockSpec((1, tl, w), lambda i, j: (i, j, 0))
    sspec = pl.BlockSpec((1, HGRN_HEADS, HGRN_HEAD_V, HGRN_EXPAND), lambda i, j: (i, 0, 0, 0))
    return pl.pallas_call(
        functools.partial(_scan_kernel, n_chunks=tl // CHUNK, rev=rev),
        grid=(b, n_tiles),
        in_specs=[tok(kw), tok(kw), tok(HGRN_WIDTH),
                  pl.BlockSpec(lb.shape, lambda i, j: (0, 0)), sspec,
                  pl.BlockSpec(ltri.shape, lambda i, j: (0, 0)),
                  pl.BlockSpec(e.shape, lambda i, j: (0, 0))],
        out_specs=[tok(HGRN_WIDTH), sspec],
        out_shape=[jax.ShapeDtypeStruct((b, t, HGRN_WIDTH), F32),
                   jax.ShapeDtypeStruct((b, HGRN_HEADS, HGRN_HEAD_V, HGRN_EXPAND), F32)],
        scratch_shapes=[pltpu.VMEM((HGRN_HEADS, HGRN_HEAD_V, HGRN_EXPAND), F32)],
        compiler_params=_cparams("arbitrary", "arbitrary"),
        name="hgrn_scan",
    )(hq, hf, hi, lb, s0, ltri, e)


def prep_out_weights(w_out):
    d = w_out.shape[1]
    wa = w_out[:MLA_HEADS * MLA_V].reshape(MLA_HEADS, MLA_V, d)
    wa = jnp.concatenate([wa, jnp.zeros((MLA_HEADS, HEAD_PAD - MLA_V, d), w_out.dtype)], axis=1)
    wa = wa.reshape(MLA_HEADS * HEAD_PAD, d)
    rest = w_out[MLA_HEADS * MLA_V:]
    return jnp.concatenate([wa, rest], axis=0).astype(BF16)


def _out_kernel(x_ref, att_ref, four_ref, of_ref, ob_ref, hg_ref, mod_ref, w_ref, gm_ref, gain_ref, x_out, h_out):
    hw = MLA_HEADS * HEAD_PAD
    o = of_ref[0] + ob_ref[0]
    ms = jnp.dot((o * o).astype(BF16), gm_ref[...], preferred_element_type=F32)
    rec = o * lax.rsqrt(ms + EPS) * gain_ref[...] * _silu(hg_ref[0])
    y = jnp.dot(att_ref[0], w_ref[:hw], preferred_element_type=F32)
    y = y + jnp.dot(four_ref[0], w_ref[hw:hw + FOURIER_WIDTH], preferred_element_type=F32)
    y = y + jnp.dot(rec.astype(BF16), w_ref[hw + FOURIER_WIDTH:], preferred_element_type=F32)
    x1 = x_ref[0] + mod_ref[0, 2:3, :] * y
    x_out[0] = x1
    h_out[0] = (_rms(x1) * (1.0 + mod_ref[0, 4:5, :]) + mod_ref[0, 3:4, :]).astype(h_out.dtype)


def out_proj(x, att, four, o_f, o_b, hg, mods, group_of_batch, w_out_ext, gain, tm):
    b, t, d = x.shape
    hw = MLA_HEADS * HEAD_PAD
    gm = jnp.asarray(np.kron(np.eye(HGRN_HEADS), np.full((HGRN_HEAD_V, HGRN_HEAD_V), 1.0 / HGRN_HEAD_V)), BF16)
    gain_t = jnp.tile(gain.astype(F32), HGRN_HEADS).reshape(1, HGRN_WIDTH)
    tok = lambda w: pl.BlockSpec((1, tm, w), lambda i, j: (i, j, 0))
    full2 = lambda a: pl.BlockSpec(a.shape, lambda i, j: (0, 0))
    if group_of_batch:
        mod_spec = pl.BlockSpec((1, N_MOD, d), lambda i, j: (i, 0, 0))
    else:
        mod_spec = pl.BlockSpec((1, N_MOD, d), lambda i, j: (0, 0, 0))
    return pl.pallas_call(
        _out_kernel,
        grid=(b, t // tm),
        in_specs=[tok(d), tok(hw), tok(FOURIER_WIDTH), tok(HGRN_WIDTH), tok(HGRN_WIDTH), tok(HGRN_WIDTH), mod_spec,
                  full2(w_out_ext), full2(gm), full2(gain_t)],
        out_specs=[tok(d), tok(d)],
        out_shape=[jax.ShapeDtypeStruct((b, t, d), F32), jax.ShapeDtypeStruct((b, t, d), BF16)],
        compiler_params=_cparams("arbitrary", "arbitrary"),
        name="out_proj",
    )(x, att, four, o_f, o_b, hg, mods, w_out_ext, gm, gain_t)


def _ffn_kernel(h_ref, x_ref, mod_ref, wg_ref, wu_ref, wd_ref, o_ref):
    j = pl.program_id(2)
    h = h_ref[0]
    a = jnp.dot(h, wg_ref[...], preferred_element_type=F32)
    u = jnp.dot(h, wu_ref[...], preferred_element_type=F32)
    y = jnp.dot((_silu(a) * u).astype(BF16), wd_ref[...], preferred_element_type=F32)
    gate = mod_ref[0, 5:6, :]

    @pl.when(j == 0)
    def _():
        o_ref[0] = x_ref[0] + gate * y

    @pl.when(j > 0)
    def _():
        o_ref[0] = o_ref[0] + gate * y


def ffn_dense(h, x, mods, group_of_batch, wg, wu, wd, tm, tf=1408):
    b, t, d = x.shape
    f = wg.shape[1]
    tok = lambda w: pl.BlockSpec((1, tm, w), lambda i, s, j: (i, s, 0))
    if group_of_batch:
        mod_spec = pl.BlockSpec((1, N_MOD, d), lambda i, s, j: (i, 0, 0))
    else:
        mod_spec = pl.BlockSpec((1, N_MOD, d), lambda i, s, j: (0, 0, 0))
    return pl.pallas_call(
        _ffn_kernel,
        grid=(b, t // tm, f // tf),
        in_specs=[tok(d), tok(d), mod_spec,
                  pl.BlockSpec((d, tf), lambda i, s, j: (0, j)),
                  pl.BlockSpec((d, tf), lambda i, s, j: (0, j)),
                  pl.BlockSpec((tf, d), lambda i, s, j: (j, 0))],
        out_specs=tok(d),
        out_shape=jax.ShapeDtypeStruct((b, t, d), F32),
        compiler_params=_cparams("arbitrary", "arbitrary", "arbitrary"),
        name="ffn_dense",
    )(h, x, mods, wg, wu, wd)


def _router_kernel(h_ref, wr_ref, g_ref):
    nt = (((1,), (1,)), ((), ()))
    logits = lax.dot_general(wr_ref[...], h_ref[0].astype(F32), nt, precision=HIGHEST, preferred_element_type=F32)
    e_id = lax.broadcasted_iota(jnp.int32, logits.shape, 0)
    m1 = jnp.max(logits, axis=0, keepdims=True)
    i1 = jnp.min(jnp.where(logits == m1, e_id, N_EXPERTS), axis=0, keepdims=True)
    rest = jnp.where(e_id == i1, -jnp.inf, logits)
    m2 = jnp.max(rest, axis=0, keepdims=True)
    i2 = jnp.min(jnp.where(rest == m2, e_id, N_EXPERTS), axis=0, keepdims=True)
    w2 = 1.0 / (1.0 + jnp.exp(m1 - m2))
    w1 = 1.0 - w2
    g_ref[0] = jnp.where(e_id == i1, w1, 0.0) + jnp.where(e_id == i2, w2, 0.0)


def router_gates(h, w_router_t, tm):
    b, t, d = h.shape
    return pl.pallas_call(
        _router_kernel,
        grid=(b, t // tm),
        in_specs=[pl.BlockSpec((1, tm, d), lambda i, j: (i, j, 0)),
                  pl.BlockSpec(w_router_t.shape, lambda i, j: (0, 0))],
        out_specs=pl.BlockSpec((1, N_EXPERTS, tm), lambda i, j: (i, 0, j)),
        out_shape=jax.ShapeDtypeStruct((b, N_EXPERTS, t), F32),
        compiler_params=_cparams("arbitrary", "arbitrary"),
        name="router",
    )(h, w_router_t)


MOE_CB = 128


def _moe_kernel(cnt_ref, h_ref, x_ref, gt_ref, g_ref, mod_ref, u_ref, wg_ref, wu_ref, wd_ref, o_ref,
                rank_scr, hc_scr, yc_scr, *, n_tiles):
    b, s, e, j = pl.program_id(0), pl.program_id(1), pl.program_id(2), pl.program_id(3)
    t = h_ref.shape[1]
    n_routed = cnt_ref[(b * n_tiles + s) * N_EXPERTS + e]
    n_blk = (n_routed + (MOE_CB - 1)) >> (MOE_CB.bit_length() - 1)
    sub = lax.broadcasted_iota(jnp.int32, (N_EXPERTS, t), 0)
    tn = (((0,), (0,)), ((), ()))

    @pl.when((e == 0) & (j == 0))
    def _():
        o_ref[0] = x_ref[0]
        sel = jnp.where(gt_ref[0] > 0.0, 1.0, 0.0).astype(BF16)
        rank_scr[...] = jnp.dot(sel, u_ref[...], preferred_element_type=F32)

    def one_hot(blk):
        gate = jnp.sum(jnp.where(sub == e, gt_ref[0], 0.0), axis=0, keepdims=True)
        rank = jnp.sum(jnp.where(sub == e, rank_scr[...], 0.0), axis=0, keepdims=True)
        rank = jnp.where(gate > 0.0, rank, -1.0)
        r = (lax.broadcasted_iota(jnp.int32, (MOE_CB, t), 0) + blk * MOE_CB).astype(F32)
        return jnp.where(r == rank, 1.0, 0.0).astype(BF16)

    @pl.when(j == 0)
    def _():
        def gather(blk, carry):
            r0 = pl.multiple_of(blk * MOE_CB, MOE_CB)
            hc = jnp.dot(one_hot(blk), h_ref[0], preferred_element_type=F32)
            hc_scr[pl.ds(r0, MOE_CB), :] = hc.astype(BF16)
            return carry
        lax.fori_loop(0, n_blk, gather, 0)

    def expert(blk, carry):
        r0 = pl.multiple_of(blk * MOE_CB, MOE_CB)
        hc = hc_scr[pl.ds(r0, MOE_CB), :]
        a = jnp.dot(hc, wg_ref[0], preferred_element_type=F32)
        u = jnp.dot(hc, wu_ref[0], preferred_element_type=F32)
        y = jnp.dot((_silu(a) * u).astype(BF16), wd_ref[0], preferred_element_type=F32)

        @pl.when(j == 0)
        def _():
            yc_scr[pl.ds(r0, MOE_CB), :] = y

        @pl.when(j > 0)
        def _():
            yc_scr[pl.ds(r0, MOE_CB), :] = yc_scr[pl.ds(r0, MOE_CB), :] + y
        return carry

    lax.fori_loop(0, n_blk, expert, 0)

    @pl.when(j == pl.num_programs(3) - 1)
    def _():
        lane = lax.broadcasted_iota(jnp.int32, g_ref.shape[1:], 1)
        ge = jnp.sum(jnp.where(lane == e, g_ref[0], 0.0), axis=1, keepdims=True)
        coef = mod_ref[0, 5:6, :] * ge

        def scatter(blk, carry):
            r0 = pl.multiple_of(blk * MOE_CB, MOE_CB)
            yc = yc_scr[pl.ds(r0, MOE_CB), :].astype(BF16)
            back = lax.dot_general(one_hot(blk), yc, tn, preferred_element_type=F32)
            o_ref[0] = o_ref[0] + coef * back
            return carry
        lax.fori_loop(0, n_blk, scatter, 0)


def moe_top2(h, x, gates, mods, group_of_batch, wg, wu, wd, tm, tf=1408):
    b, t, d = x.shape
    f = wg.shape[2]
    n_tiles = t // tm
    counts = jnp.sum((gates > 0.0).reshape(b, N_EXPERTS, n_tiles, tm), axis=-1, dtype=jnp.int32)
    counts = counts.transpose(0, 2, 1).reshape(-1)
    gates_lane = jnp.pad(gates.transpose(0, 2, 1), ((0, 0), (0, 0), (0, HEAD_PAD - N_EXPERTS)))
    r = np.arange(tm)
    upper = jnp.asarray(r[:, None] < r[None, :], BF16)
    tok = lambda w: pl.BlockSpec((1, tm, w), lambda i, s, e, j, c: (i, s, 0))
    if group_of_batch:
        mod_spec = pl.BlockSpec((1, N_MOD, d), lambda i, s, e, j, c: (i, 0, 0))
    else:
        mod_spec = pl.BlockSpec((1, N_MOD, d), lambda i, s, e, j, c: (0, 0, 0))
    grid_spec = pltpu.PrefetchScalarGridSpec(
        num_scalar_prefetch=1,
        grid=(b, n_tiles, N_EXPERTS, f // tf),
        in_specs=[tok(d), tok(d),
                  pl.BlockSpec((1, N_EXPERTS, tm), lambda i, s, e, j, c: (i, 0, s)),
                  tok(HEAD_PAD), mod_spec,
                  pl.BlockSpec((tm, tm), lambda i, s, e, j, c: (0, 0)),
                  pl.BlockSpec((1, d, tf), lambda i, s, e, j, c: (e, 0, j)),
                  pl.BlockSpec((1, d, tf), lambda i, s, e, j, c: (e, 0, j)),
                  pl.BlockSpec((1, tf, d), lambda i, s, e, j, c: (e, j, 0))],
        out_specs=tok(d),
        scratch_shapes=[pltpu.VMEM((N_EXPERTS, tm), F32), pltpu.VMEM((tm, d), BF16), pltpu.VMEM((tm, d), F32)],
    )
    return pl.pallas_call(
        functools.partial(_moe_kernel, n_tiles=n_tiles),
        grid_spec=grid_spec,
        out_shape=jax.ShapeDtypeStruct((b, t, d), F32),
        compiler_params=_cparams("arbitrary", "arbitrary", "arbitrary", "arbitrary"),
        name="moe_top2",
    )(counts, h, x, gates, gates_lane, mods, upper, wg, wu, wd)


def _final_kernel(x_ref, g_ref, o_ref):
    o_ref[0] = _rms(x_ref[0]) * g_ref[...]


def final_norm(x, gain, tm=1024):
    b, t, d = x.shape
    return pl.pallas_call(
        _final_kernel,
        grid=(b, t // tm),
        in_specs=[pl.BlockSpec((1, tm, d), lambda i, j: (i, j, 0)), pl.BlockSpec((1, d), lambda i, j: (0, 0))],
        out_specs=pl.BlockSpec((1, tm, d), lambda i, j: (i, j, 0)),
        out_shape=jax.ShapeDtypeStruct((b, t, d), F32),
        compiler_params=_cparams("arbitrary", "arbitrary"),
        name="final_norm",
    )(x, gain.reshape(1, d))


def _lower_bounds(lb_logits):
    p = jax.nn.softmax(lb_logits.astype(F32), axis=1)
    return jnp.cumsum(p, axis=1) - p[:, :1]


def _bidirectional_scan(hq, hf, hb, hi, lb_f, lb_b, s0_f, s0_b):
    o_f, s_f = hgrn_scan(hq, hf, hi, lb_f, s0_f, False)
    o_b, s_b = hgrn_scan(hq, hb, hi, lb_b, s0_b, True)
    return o_f, o_b, s_f, s_b


def _channel_mix(layer, h, x, mods, by_batch, ffn_w, moe_w, tm):
    j = layer // 2
    if layer % 2 == 0:
        wg, wu, wd = ffn_w
        return ffn_dense(h, x, mods, by_batch, wg[j].astype(BF16), wu[j].astype(BF16), wd[j].astype(BF16), tm)
    router, wg, wu, wd = moe_w
    gates = router_gates(h, router[j].T.astype(F32), tm)
    return moe_top2(h, x, gates, mods, by_batch, wg[j].astype(BF16), wu[j].astype(BF16), wd[j].astype(BF16),
                    min(x.shape[1], 1024))


def kernel(x, c, ctx, c_ctx, w_ada, b_ada, w_in, q_norm_g, kv_norm_g, w_uq, w_ukv, hgrn_norm_g, lb_logits, w_out,
           ffn_w_gate, ffn_w_up, ffn_w_down, moe_router, moe_w_gate, moe_w_up, moe_w_down, final_g):
    bsz, n_lat, d = x.shape
    n_ctx = ctx.shape[1]
    depth = w_ada.shape[0]
    x = x.astype(F32)
    ctx = ctx.astype(F32)

    n_groups = 8
    cond = jnp.zeros((n_groups, d), F32).at[:bsz].set(c.astype(F32)).at[bsz].set(c_ctx.astype(F32))
    mods_all = ada_mods(cond, w_ada.astype(F32), b_ada.astype(F32)).reshape(depth, n_groups, N_MOD, d)
    lbs = _lower_bounds(lb_logits)
    tabs_lat = rope_tables(n_lat)
    tabs_ctx = identity_rope_tables(n_ctx)
    zero_state = jnp.zeros((bsz, HGRN_HEADS, HGRN_HEAD_V, HGRN_EXPAND), F32)
    ffn_w = (ffn_w_gate, ffn_w_up, ffn_w_down)
    moe_w = (moe_router, moe_w_gate, moe_w_up, moe_w_down)
    tm_lat = 512
    tm_ctx = n_ctx

    for layer in range(depth):
        need_ctx = layer < depth - 1
        mods_lat = mods_all[layer, :bsz]
        mods_ctx = mods_all[layer, bsz:bsz + 1]
        w_ext, wq_ext, wkv_ext = prep_in_weights(w_in[layer], w_uq[layer], w_ukv[layer])
        gq = q_norm_g[layer].astype(F32).reshape(1, -1)
        gkv = kv_norm_g[layer].astype(F32).reshape(1, -1)
        lb_f = lbs[0, layer].reshape(1, -1)
        lb_b = lbs[1, layer].reshape(1, -1)
        w_out_ext = prep_out_weights(w_out[layer])

        q_c, k_c, v_c, fo_c, hq_c, hf_c, hb_c, hi_c, hg_c = in_proj(
            ctx, mods_ctx, False, w_ext, gq, gkv, wq_ext, wkv_ext, tabs_ctx, tm_ctx)
        q_l, k_l, v_l, fo_l, hq_l, hf_l, hb_l, hi_l, hg_l = in_proj(
            x, mods_lat, True, w_ext, gq, gkv, wq_ext, wkv_ext, tabs_lat, tm_lat)

        att_l = attention(q_l, k_c, v_c, k_l, v_l)
        four_l = fourier_large(fo_l)
        oc_f, oc_b, sc_f, sc_b = _bidirectional_scan(hq_c, hf_c, hb_c, hi_c, lb_f, lb_b, zero_state, zero_state)
        ol_f, ol_b, _, _ = _bidirectional_scan(hq_l, hf_l, hb_l, hi_l, lb_f, lb_b, sc_f, sc_b)
        x, h_l = out_proj(x, att_l, four_l, ol_f, ol_b, hg_l, mods_lat, True, w_out_ext, hgrn_norm_g[layer], tm_lat)
        if need_ctx:
            att_c = attention(q_c, k_c, v_c)
            four_c = fourier_small(fo_c)
            ctx, h_c = out_proj(ctx, att_c, four_c, oc_f, oc_b, hg_c, mods_ctx, False, w_out_ext,
                                hgrn_norm_g[layer], tm_ctx)
            ctx = _channel_mix(layer, h_c, ctx, mods_ctx, False, ffn_w, moe_w, tm_ctx)
        x = _channel_mix(layer, h_l, x, mods_lat, True, ffn_w, moe_w, tm_lat)

    return final_norm(x, final_g.astype(F32))
```
